```python
import math
import jax, jax.numpy as jnp
from jax import lax
import numpy as np

D_MODEL = 1024
BATCH = 32
SEQ = 2048
DEPTH = 1
DEC_BATCH = 2
DEC_SEQ = 8192
PAST_LEN = 128

N_HEADS = 8
QK_HEAD_DIM = 64
V_HEAD_DIM = 2 * QK_HEAD_DIM
Q_WIDTH = N_HEADS * 2 * QK_HEAD_DIM
ATTN_WIDTH = N_HEADS * V_HEAD_DIM
N_FOURIER_GROUPS = 4
FOURIER_GROUP_DIM = 128
FOURIER_WIDTH = N_FOURIER_GROUPS * FOURIER_GROUP_DIM
ROPE_DIM = QK_HEAD_DIM // 4
ROPE_THETA = 500000.0
D_FF = 4 * D_MODEL
PLE_DIM = 256
Q_BLOCK = 128
EPS = 1e-6
LAMBDA_STD = 0.1
IN_WIDTH = FOURIER_WIDTH + Q_WIDTH + Q_WIDTH + ATTN_WIDTH + D_MODEL + D_MODEL

kernel_name = "hybrid_fnet_diffattn_encoder"


def rmsnorm(x, g):
    xf = x.astype(jnp.float32)
    y = xf * lax.rsqrt(jnp.mean(xf * xf, axis=-1, keepdims=True) + EPS)
    return (y * g.astype(jnp.float32)).astype(x.dtype)


def partial_rope(x):
    S = x.shape[1]
    half = ROPE_DIM // 2
    pos = jnp.arange(S, dtype=jnp.float32)
    inv_freq = ROPE_THETA ** (-(jnp.arange(0, ROPE_DIM, 2, dtype=jnp.float32) / ROPE_DIM))
    ang = pos[:, None] * inv_freq[None, :]
    cos = jnp.cos(ang)[None, :, None, None, :].astype(x.dtype)
    sin = jnp.sin(ang)[None, :, None, None, :].astype(x.dtype)
    x1 = x[..., :half]
    x2 = x[..., half:ROPE_DIM]
    rot = jnp.concatenate([x1 * cos - x2 * sin, x2 * cos + x1 * sin], axis=-1)
    return jnp.concatenate([rot, x[..., ROPE_DIM:]], axis=-1)


def fourier_mix(f):
    B, S, _ = f.shape
    fg = f.astype(jnp.float32).reshape(B, S, N_FOURIER_GROUPS, FOURIER_GROUP_DIM)
    out = jnp.fft.fft2(fg, axes=(1, 3), norm="ortho").real
    return out.reshape(B, S, FOURIER_WIDTH).astype(f.dtype)


def diff_attention(q, k, v, lam):
    B, S, H, _, Dh = q.shape
    nb = S // Q_BLOCK
    scale = 1.0 / math.sqrt(Dh)
    qb = q.reshape(B, nb, Q_BLOCK, H, 2, Dh).transpose(1, 0, 2, 3, 4, 5)

    def block(qi):
        s = jnp.einsum('bqhcd,bkhcd->bchqk', qi, k, preferred_element_type=jnp.float32) * scale
        pr = jax.nn.softmax(s, axis=-1)
        a = pr[:, 0] - lam * pr[:, 1]
        return jnp.einsum('bhqk,bkhd->bqhd', a.astype(v.dtype), v)

    out = lax.map(block, qb)
    return out.transpose(1, 0, 2, 3, 4).reshape(B, S, H, V_HEAD_DIM)


def layer(x, p, layer_idx, norm_mix_pre, w_in, w_fourier, w_attn, w_out,
          lambda_q1, lambda_k1, lambda_q2, lambda_k2, subln,
          norm_mix_post, norm_mlp_pre, w_up, w_down, norm_mlp_post,
          w_ple, w_ple_gate, norm_ple_post):
    B, S, _ = x.shape
    h = rmsnorm(x, norm_mix_pre)
    proj = h @ w_in
    o = 0
    f_in = proj[..., o:o + FOURIER_WIDTH]; o += FOURIER_WIDTH
    q = proj[..., o:o + Q_WIDTH]; o += Q_WIDTH
    k = proj[..., o:o + Q_WIDTH]; o += Q_WIDTH
    v = proj[..., o:o + ATTN_WIDTH]; o += ATTN_WIDTH
    g_f = jax.nn.sigmoid(proj[..., o:o + D_MODEL]); o += D_MODEL
    g_a = jax.nn.sigmoid(proj[..., o:o + D_MODEL])

    fourier_out = fourier_mix(f_in) @ w_fourier

    q = partial_rope(q.reshape(B, S, N_HEADS, 2, QK_HEAD_DIM))
    k = partial_rope(k.reshape(B, S, N_HEADS, 2, QK_HEAD_DIM))
    v = v.reshape(B, S, N_HEADS, V_HEAD_DIM)
    lam_init = 0.8 - 0.6 * math.exp(-0.3 * layer_idx)
    lam = (jnp.exp(jnp.sum(lambda_q1.astype(jnp.float32) * lambda_k1.astype(jnp.float32)))
           - jnp.exp(jnp.sum(lambda_q2.astype(jnp.float32) * lambda_k2.astype(jnp.float32)))
           + lam_init)
    att = diff_attention(q, k, v, lam)
    att = rmsnorm(att, subln) * (1.0 - lam_init)
    attn_out = att.reshape(B, S, ATTN_WIDTH) @ w_attn

    merged = g_f * fourier_out + g_a * attn_out
    x = x + rmsnorm(merged @ w_out, norm_mix_post)

    h2 = rmsnorm(x, norm_mlp_pre)
    u = jnp.square(jax.nn.relu(h2 @ w_up))
    x = x + rmsnorm(u @ w_down, norm_mlp_post)

    e = (p @ w_ple) * jax.nn.sigmoid(x @ w_ple_gate)
    x = x + rmsnorm(e, norm_ple_post)
    return x


def setup_inputs(seed: int = 0) -> dict:
    key = jax.random.key(seed)
    ks = jax.random.split(key, 24)
    f32 = jnp.float32

    def nrm(k, shape, scale=1.0):
        return jax.random.normal(k, shape, dtype=f32) * scale

    def gain(k):
        return 1.0 + 0.01 * jax.random.normal(k, (DEPTH, D_MODEL), dtype=f32)

    return {
        "x_prompt": nrm(ks[0], (BATCH, SEQ, D_MODEL)),
        "x_sample": nrm(ks[1], (DEC_BATCH, DEC_SEQ, D_MODEL)),
        "p_prompt": nrm(ks[2], (DEPTH, BATCH, SEQ, PLE_DIM)),
        "p_sample": nrm(ks[3], (DEPTH, DEC_BATCH, DEC_SEQ, PLE_DIM)),
        "norm_mix_pre": gain(ks[4]),
        "w_in": nrm(ks[5], (DEPTH, D_MODEL, IN_WIDTH), D_MODEL ** -0.5),
        "w_fourier": nrm(ks[6], (DEPTH, FOURIER_WIDTH, D_MODEL), FOURIER_WIDTH ** -0.5),
        "w_attn": nrm(ks[7], (DEPTH, ATTN_WIDTH, D_MODEL), ATTN_WIDTH ** -0.5),
        "w_out": nrm(ks[8], (DEPTH, D_MODEL, D_MODEL), D_MODEL ** -0.5),
        "lambda_q1": nrm(ks[9], (DEPTH, QK_HEAD_DIM), LAMBDA_STD),
        "lambda_k1": nrm(ks[10], (DEPTH, QK_HEAD_DIM), LAMBDA_STD),
        "lambda_q2": nrm(ks[11], (DEPTH, QK_HEAD_DIM), LAMBDA_STD),
        "lambda_k2": nrm(ks[12], (DEPTH, QK_HEAD_DIM), LAMBDA_STD),
        "subln": 1.0 + 0.01 * nrm(ks[13], (DEPTH, V_HEAD_DIM)),
        "norm_mix_post": gain(ks[14]),
        "norm_mlp_pre": gain(ks[15]),
        "w_up": nrm(ks[16], (DEPTH, D_MODEL, D_FF), D_MODEL ** -0.5),
        "w_down": nrm(ks[17], (DEPTH, D_FF, D_MODEL), D_FF ** -0.5),
        "norm_mlp_post": gain(ks[18]),
        "w_ple": nrm(ks[19], (DEPTH, PLE_DIM, D_MODEL), PLE_DIM ** -0.5),
        "w_ple_gate": nrm(ks[20], (DEPTH, D_MODEL, D_MODEL), D_MODEL ** -0.5),
        "norm_ple_post": gain(ks[21]),
    }


def reference(x_prompt, x_sample, p_prompt, p_sample, norm_mix_pre, w_in, w_fourier, w_attn, w_out,
              lambda_q1, lambda_k1, lambda_q2, lambda_k2, subln, norm_mix_post, norm_mlp_pre,
              w_up, w_down, norm_mlp_post, w_ple, w_ple_gate, norm_ple_post):
    y_prompt = x_prompt
    y_sample = x_sample
    for i in range(DEPTH):
        params = (norm_mix_pre[i], w_in[i], w_fourier[i], w_attn[i], w_out[i],
                  lambda_q1[i], lambda_k1[i], lambda_q2[i], lambda_k2[i], subln[i],
                  norm_mix_post[i], norm_mlp_pre[i], w_up[i], w_down[i], norm_mlp_post[i],
                  w_ple[i], w_ple_gate[i], norm_ple_post[i])
        y_prompt = layer(y_prompt, p_prompt[i], i, *params)
        y_sample = layer(y_sample, p_sample[i], i, *params)
    return (y_prompt, y_sample)
```

```python
import functools
import math

import jax
import jax.numpy as jnp
import numpy as np
from jax import lax
from jax.experimental import pallas as pl
from jax.experimental.pallas import tpu as pltpu

D_MODEL = 1024
N_HEADS = 8
QK_HEAD_DIM = 64
V_HEAD_DIM = 128
Q_WIDTH = N_HEADS * 2 * QK_HEAD_DIM
ATTN_WIDTH = N_HEADS * V_HEAD_DIM
N_FOURIER_GROUPS = 4
FOURIER_GROUP_DIM = 128
FOURIER_WIDTH = N_FOURIER_GROUPS * FOURIER_GROUP_DIM
ROPE_DIM = QK_HEAD_DIM // 4
ROPE_THETA = 500000.0
D_FF = 4 * D_MODEL
PLE_DIM = 256
EPS = 1e-6

LANES = 128
VMEM_LIMIT = 56 * 1024 * 1024
F32 = jnp.float32
BF16 = jnp.bfloat16
LOG2E = 1.4426950408889634


def _rms(x, g):
    return x * lax.rsqrt(jnp.mean(x * x, axis=-1, keepdims=True) + EPS) * g


def _dot(a, b):
    return jnp.dot(a, b, preferred_element_type=F32)


def _sigmoid(x):
    return 1.0 / (1.0 + jnp.exp(-x))


def _const_spec(shape):
    nd = len(shape)
    return pl.BlockSpec(shape, lambda *_: (0,) * nd, pipeline_mode=pl.Buffered(1))


def _rope(y, c, sa, sb):
    outs = []
    for j in range(y.shape[1] // LANES):
        yc = y[:, j * LANES:(j + 1) * LANES]
        outs.append(yc * c + pltpu.roll(yc, ROPE_DIM // 2, 1) * sa
                    + pltpu.roll(yc, LANES - ROPE_DIM // 2, 1) * sb)
    return jnp.concatenate(outs, axis=1)


def _in_proj_kernel(x_ref, g_ref, w_ref, cq_ref, sqa_ref, sqb_ref, ck_ref, ska_ref, skb_ref,
                    f_ref, q_ref, k_ref, v_ref, gf_ref, ga_ref):
    h = _rms(x_ref[...], g_ref[...]).astype(BF16)
    o = 0
    f_ref[...] = _dot(h, w_ref[:, o:o + FOURIER_WIDTH]).astype(BF16); o += FOURIER_WIDTH
    q = _dot(h, w_ref[:, o:o + Q_WIDTH]); o += Q_WIDTH
    q_ref[...] = _rope(q, cq_ref[...], sqa_ref[...], sqb_ref[...]).astype(BF16)
    k = _dot(h, w_ref[:, o:o + Q_WIDTH]); o += Q_WIDTH
    k_ref[...] = _rope(k, ck_ref[...], ska_ref[...], skb_ref[...]).astype(BF16)
    v_ref[...] = _dot(h, w_ref[:, o:o + ATTN_WIDTH]).astype(BF16); o += ATTN_WIDTH
    gf_ref[...] = _sigmoid(_dot(h, w_ref[:, o:o + D_MODEL])).astype(BF16); o += D_MODEL
    ga_ref[...] = _sigmoid(_dot(h, w_ref[:, o:o + D_MODEL])).astype(BF16)


def _rope_tables(seq, scale):
    half = ROPE_DIM // 2
    pos = jnp.arange(seq, dtype=F32)
    inv_freq = ROPE_THETA ** (-(jnp.arange(0, ROPE_DIM, 2, dtype=F32) / ROPE_DIM))
    ang = pos[:, None] * inv_freq[None, :]
    cos, sin = jnp.cos(ang), jnp.sin(ang)
    ones = jnp.ones((seq, QK_HEAD_DIM - ROPE_DIM), F32)
    zeros = jnp.zeros((seq, QK_HEAD_DIM - ROPE_DIM), F32)
    zh = jnp.zeros((seq, half), F32)
    c = jnp.concatenate([cos, cos, ones], axis=1)
    sa = jnp.concatenate([zh, sin, zeros], axis=1)
    sb = jnp.concatenate([-sin, zh, zeros], axis=1)
    rep = LANES // QK_HEAD_DIM
    return tuple(jnp.tile(t * scale, (1, rep)) for t in (c, sa, sb))


def _in_proj(x2d, gain, w_in, seq, tm):
    n_tok = x2d.shape[0]
    q_scale = LOG2E / math.sqrt(QK_HEAD_DIM)
    tabs = _rope_tables(seq, q_scale) + _rope_tables(seq, 1.0)
    spt = seq // tm
    tok = lambda w: pl.BlockSpec((tm, w), lambda i: (i, 0))
    tab = pl.BlockSpec((tm, LANES), lambda i: (i % spt, 0))
    widths = (FOURIER_WIDTH, Q_WIDTH, Q_WIDTH, ATTN_WIDTH, D_MODEL, D_MODEL)
    return pl.pallas_call(
        _in_proj_kernel,
        grid=(n_tok // tm,),
        in_specs=[tok(D_MODEL), _const_spec((1, D_MODEL)), _const_spec(w_in.shape)] + [tab] * 6,
        out_specs=[tok(w) for w in widths],
        out_shape=[jax.ShapeDtypeStruct((n_tok, w), BF16) for w in widths],
        compiler_params=pltpu.CompilerParams(dimension_semantics=("parallel",),
                                             vmem_limit_bytes=VMEM_LIMIT),
        name="in_proj",
    )(x2d, gain, w_in, *tabs)


def _chan_dft_kernel(f_ref, w_ref, o_ref):
    ab = _dot(f_ref[0], w_ref[...])
    o_ref[0] = ab[:, :FOURIER_WIDTH].astype(BF16)
    o_ref[1] = ab[:, FOURIER_WIDTH:].astype(BF16)


def _chan_dft_weights():
    n = FOURIER_GROUP_DIM
    idx = np.arange(n)
    ang = 2.0 * np.pi * ((idx[:, None] * idx[None, :]) % n) / n
    eye = np.eye(N_FOURIER_GROUPS)
    c = np.kron(eye, np.cos(ang)) / math.sqrt(n)
    s = np.kron(eye, np.sin(ang)) / math.sqrt(n)
    return jnp.asarray(np.concatenate([c, s], axis=1), dtype=BF16)


def _chan_dft(f3d, tm):
    b, seq, _ = f3d.shape
    return pl.pallas_call(
        _chan_dft_kernel,
        grid=(b, seq // tm),
        in_specs=[pl.BlockSpec((1, tm, FOURIER_WIDTH), lambda bi, i: (bi, i, 0)),
                  _const_spec((FOURIER_WIDTH, 2 * FOURIER_WIDTH))],
        out_specs=pl.BlockSpec((2, tm, FOURIER_WIDTH), lambda bi, i: (0, i, bi)),
        out_shape=jax.ShapeDtypeStruct((2, seq, b * FOURIER_WIDTH), BF16),
        compiler_params=pltpu.CompilerParams(dimension_semantics=("parallel", "parallel"),
                                             vmem_limit_bytes=VMEM_LIMIT),
        name="chan_dft",
    )(f3d, _chan_dft_weights())


def _seq_dft_table(seq):
    j = jnp.arange(seq, dtype=jnp.int32)
    ph = (j[:, None] * j[None, :]) % seq
    ang = ph.astype(F32) * (2.0 * math.pi / seq)
    sc = 1.0 / math.sqrt(seq)
    return jnp.concatenate([jnp.cos(ang) * sc, jnp.sin(ang) * (-sc)], axis=1).astype(BF16)


def _matmul_kernel(a_ref, b_ref, o_ref, acc_ref):
    kk = pl.program_id(2)

    @pl.when(kk == 0)
    def _():
        acc_ref[...] = jnp.zeros_like(acc_ref)

    acc_ref[...] += _dot(a_ref[...], b_ref[...])

    @pl.when(kk == pl.num_programs(2) - 1)
    def _():
        o_ref[...] = acc_ref[...].astype(o_ref.dtype)


def _matmul(a, b, tm, tn, tk):
    m, k = a.shape
    _, n = b.shape
    return pl.pallas_call(
        _matmul_kernel,
        grid=(m // tm, n // tn, k // tk),
        in_specs=[pl.BlockSpec((tm, tk), lambda i, j, kk: (i, kk)),
                  pl.BlockSpec((tk, tn), lambda i, j, kk: (kk, j))],
        out_specs=pl.BlockSpec((tm, tn), lambda i, j, kk: (i, j)),
        out_shape=jax.ShapeDtypeStruct((m, n), BF16),
        scratch_shapes=[pltpu.VMEM((tm, tn), F32)],
        compiler_params=pltpu.CompilerParams(
            dimension_semantics=("parallel", "parallel", "arbitrary"),
            vmem_limit_bytes=VMEM_LIMIT),
        name="seq_dft",
    )(a, b)


def _lambda_kernel(lam_init, q1_ref, k1_ref, q2_ref, k2_ref, o_ref):
    s1 = jnp.sum(q1_ref[...] * k1_ref[...], axis=-1, keepdims=True)
    s2 = jnp.sum(q2_ref[...] * k2_ref[...], axis=-1, keepdims=True)
    o_ref[...] = jnp.exp(s1) - jnp.exp(s2) + lam_init


def _lambda(lam_init, q1, k1, q2, k2):
    vec = lambda a: a.reshape(1, QK_HEAD_DIM).astype(F32)
    return pl.pallas_call(
        functools.partial(_lambda_kernel, lam_init),
        out_shape=jax.ShapeDtypeStruct((1, 1), F32),
        name="diff_lambda",
    )(vec(q1), vec(k1), vec(q2), vec(k2))


def _attn_kernel(tq, tk, out_scale, lam_ref, q_ref, k_ref, v_ref, sub_ref, o_ref):
    seq = q_ref.shape[1]
    lam = lam_ref[0, 0]
    lane = lax.broadcasted_iota(jnp.int32, (1, V_HEAD_DIM), 1)
    first = lane < QK_HEAD_DIM
    sub = sub_ref[...] * out_scale
    nt = (((1,), (1,)), ((), ()))

    def q_body(qi, _):
        row = pl.multiple_of(qi * tq, tq)
        q = q_ref[0, pl.ds(row, tq), :]
        zero = jnp.zeros_like(q)
        qz = jnp.concatenate([jnp.where(first, q, zero), jnp.where(first, zero, q)], axis=0)

        def kv_body(j, carry):
            m, l, acc = carry
            col = pl.multiple_of(j * tk, tk)
            kj = k_ref[0, pl.ds(col, tk), :]
            vj = v_ref[0, pl.ds(col, tk), :]
            s = lax.dot_general(qz, kj, nt, preferred_element_type=F32)
            m_new = jnp.maximum(m, jnp.max(s, axis=-1, keepdims=True))
            alpha = jnp.exp2(m - m_new)
            p = jnp.exp2(s - m_new)
            l = alpha * l + jnp.sum(p, axis=-1, keepdims=True)
            acc = alpha * acc + _dot(p.astype(BF16), vj)
            return m_new, l, acc

        init = (jnp.full((2 * tq, 1), -jnp.inf, F32), jnp.zeros((2 * tq, 1), F32),
                jnp.zeros((2 * tq, V_HEAD_DIM), F32))
        _, l, acc = lax.fori_loop(0, seq // tk, kv_body, init)
        o = acc / l
        att = o[:tq] - lam * o[tq:]
        o_ref[0, pl.ds(row, tq), :] = _rms(att, sub).astype(BF16)
        return 0

    lax.fori_loop(0, seq // tq, q_body, 0)


def _diff_attn(lam, q, k, v, subln, out_scale, tq, tk):
    b, seq, _ = q.shape
    head = pl.BlockSpec((1, seq, V_HEAD_DIM), lambda bi, h: (bi, 0, h))
    return pl.pallas_call(
        functools.partial(_attn_kernel, tq, tk, out_scale),
        grid=(b, N_HEADS),
        in_specs=[pl.BlockSpec(memory_space=pltpu.SMEM), head, head, head,
                  _const_spec((1, V_HEAD_DIM))],
        out_specs=head,
        out_shape=jax.ShapeDtypeStruct((b, seq, ATTN_WIDTH), BF16),
        compiler_params=pltpu.CompilerParams(dimension_semantics=("parallel", "parallel"),
                                             vmem_limit_bytes=VMEM_LIMIT),
        name="diff_attn",
    )(lam, q, k, v, subln)


def _merge_kernel(x_ref, y_ref, att_ref, gf_ref, ga_ref, wf_ref, wa_ref, wo_ref, g_ref, o_ref):
    fo = _dot(y_ref[...], wf_ref[...])
    ao = _dot(att_ref[...], wa_ref[...])
    merged = gf_ref[...].astype(F32) * fo + ga_ref[...].astype(F32) * ao
    o_ref[...] = x_ref[...] + _rms(_dot(merged.astype(BF16), wo_ref[...]), g_ref[...])


def _merge(x2d, y2, att2d, gf, ga, w_fourier, w_attn, w_out, gain, seq, tm):
    n_tok = x2d.shape[0]
    spt = seq // tm
    tok = lambda w: pl.BlockSpec((tm, w), lambda i: (i, 0))
    return pl.pallas_call(
        _merge_kernel,
        grid=(n_tok // tm,),
        in_specs=[tok(D_MODEL),
                  pl.BlockSpec((tm, FOURIER_WIDTH), lambda i: (i % spt, i // spt)),
                  tok(ATTN_WIDTH), tok(D_MODEL), tok(D_MODEL),
                  _const_spec(w_fourier.shape), _const_spec(w_attn.shape),
                  _const_spec(w_out.shape), _const_spec((1, D_MODEL))],
        out_specs=tok(D_MODEL),
        out_shape=jax.ShapeDtypeStruct((n_tok, D_MODEL), F32),
        compiler_params=pltpu.CompilerParams(dimension_semantics=("parallel",),
                                             vmem_limit_bytes=VMEM_LIMIT),
        name="merge",
    )(x2d, y2, att2d, gf, ga, w_fourier, w_attn, w_out, gain)


def _mlp_ple_kernel(x_ref, p_ref, gpre_ref, wup_ref, wdn_ref, gpost_ref, wple_ref, wgate_ref,
                    gple_ref, o_ref):
    x = x_ref[...]
    h = _rms(x, gpre_ref[...]).astype(BF16)
    d = jnp.zeros(x.shape, F32)
    for c in range(D_FF // D_MODEL):
        cols = slice(c * D_MODEL, (c + 1) * D_MODEL)
        u = jnp.square(jnp.maximum(_dot(h, wup_ref[:, cols]), 0.0))
        d = d + _dot(u.astype(BF16), wdn_ref[cols, :])
    x = x + _rms(d, gpost_ref[...])
    e = _dot(p_ref[...].astype(BF16), wple_ref[...]) * _sigmoid(_dot(x.astype(BF16), wgate_ref[...]))
    o_ref[...] = x + _rms(e, gple_ref[...])


def _mlp_ple(x2d, p2d, g_pre, w_up, w_down, g_post, w_ple, w_gate, g_ple, tm):
    n_tok = x2d.shape[0]
    tok = lambda w: pl.BlockSpec((tm, w), lambda i: (i, 0))
    vec = _const_spec((1, D_MODEL))
    return pl.pallas_call(
        _mlp_ple_kernel,
        grid=(n_tok // tm,),
        in_specs=[tok(D_MODEL), tok(PLE_DIM), vec, _const_spec(w_up.shape),
                  _const_spec(w_down.shape), vec, _const_spec(w_ple.shape),
                  _const_spec(w_gate.shape), vec],
        out_specs=tok(D_MODEL),
        out_shape=jax.ShapeDtypeStruct((n_tok, D_MODEL), F32),
        compiler_params=pltpu.CompilerParams(dimension_semantics=("parallel",),
                                             vmem_limit_bytes=VMEM_LIMIT),
        name="mlp_ple",
    )(x2d, p2d, g_pre, w_up, w_down, g_post, w_ple, w_gate, g_ple)


def _layer(x, p, layer_idx, prm):
    (norm_mix_pre, w_in, w_fourier, w_attn, w_out, lq1, lk1, lq2, lk2, subln,
     norm_mix_post, norm_mlp_pre, w_up, w_down, norm_mlp_post, w_ple, w_gate, norm_ple_post) = prm
    b, seq, _ = x.shape
    n_tok = b * seq
    row = lambda g: g.reshape(1, -1).astype(F32)
    x2d = x.reshape(n_tok, D_MODEL)

    f, q, k, v, gf, ga = _in_proj(x2d, row(norm_mix_pre), w_in.astype(BF16), seq, tm=512)

    ab = _chan_dft(f.reshape(b, seq, FOURIER_WIDTH), tm=512)
    y2 = _matmul(_seq_dft_table(seq), ab.reshape(2 * seq, b * FOURIER_WIDTH), 1024, 1024, 1024)

    lam_init = 0.8 - 0.6 * math.exp(-0.3 * layer_idx)
    lam = _lambda(lam_init, lq1, lk1, lq2, lk2)
    shp = (b, seq, Q_WIDTH)
    att = _diff_attn(lam, q.reshape(shp), k.reshape(shp), v.reshape(shp), row(subln),
                     1.0 - lam_init, tq=256, tk=512)

    x1 = _merge(x2d, y2, att.reshape(n_tok, ATTN_WIDTH), gf, ga, w_fourier.astype(BF16),
                w_attn.astype(BF16), w_out.astype(BF16), row(norm_mix_post), seq, tm=512)
    out = _mlp_ple(x1, p.reshape(n_tok, PLE_DIM), row(norm_mlp_pre), w_up.astype(BF16),
                   w_down.astype(BF16), row(norm_mlp_post), w_ple.astype(BF16),
                   w_gate.astype(BF16), row(norm_ple_post), tm=256)
    return out.reshape(b, seq, D_MODEL)


def kernel(x_prompt, x_sample, p_prompt, p_sample, norm_mix_pre, w_in, w_fourier, w_attn, w_out,
           lambda_q1, lambda_k1, lambda_q2, lambda_k2, subln, norm_mix_post, norm_mlp_pre,
           w_up, w_down, norm_mlp_post, w_ple, w_ple_gate, norm_ple_post):
    y_prompt, y_sample = x_prompt, x_sample
    for i in range(w_in.shape[0]):
        prm = (norm_mix_pre[i], w_in[i], w_fourier[i], w_attn[i], w_out[i],
               lambda_q1[i], lambda_k1[i], lambda_q2[i], lambda_k2[i], subln[i],
               norm_mix_post[i], norm_mlp_pre[i], w_up[i], w_down[i], norm_mlp_post[i],
               w_ple[i], w_ple_gate[i], norm_ple_post[i])
        y_prompt = _layer(y_prompt, p_prompt[i], i, prm)
        y_sample = _layer(y_sample, p_sample[i], i, prm)
    return (y_prompt, y_sample)
```

```python
import functools
import math

import jax
import jax.numpy as jnp
import numpy as np
from jax import lax
from jax.experimental import pallas as pl
from jax.experimental.pallas import tpu as pltpu

D_MODEL = 1024
N_HEADS = 8
QK_HEAD_DIM = 64
V_HEAD_DIM = 128
Q_WIDTH = N_HEADS * 2 * QK_HEAD_DIM
ATTN_WIDTH = N_HEADS * V_HEAD_DIM
N_FOURIER_GROUPS = 4
FOURIER_GROUP_DIM = 128
FOURIER_WIDTH = N_FOURIER_GROUPS * FOURIER_GROUP_DIM
ROPE_DIM = QK_HEAD_DIM // 4
ROPE_THETA = 500000.0
D_FF = 4 * D_MODEL
PLE_DIM = 256
EPS = 1e-6

LANES = 128
VMEM_LIMIT = 56 * 1024 * 1024
F32 = jnp.float32
BF16 = jnp.bfloat16
LOG2E = 1.4426950408889634


def _rms(x, g):
    return x * lax.rsqrt(jnp.mean(x * x, axis=-1, keepdims=True) + EPS) * g


def _dot(a, b):
    return jnp.dot(a, b, preferred_element_type=F32)


def _sigmoid(x):
    return 1.0 / (1.0 + jnp.exp(-x))


def _const_spec(shape):
    nd = len(shape)
    return pl.BlockSpec(shape, lambda *_: (0,) * nd, pipeline_mode=pl.Buffered(1))


def _rope(y, c, sa, sb):
    outs = []
    for j in range(y.shape[1] // LANES):
        yc = y[:, j * LANES:(j + 1) * LANES]
        outs.append(yc * c + pltpu.roll(yc, ROPE_DIM // 2, 1) * sa
                    + pltpu.roll(yc, LANES - ROPE_DIM // 2, 1) * sb)
    return jnp.concatenate(outs, axis=1)


def _in_proj_kernel(x_ref, g_ref, w_ref, cq_ref, sqa_ref, sqb_ref, ck_ref, ska_ref, skb_ref,
                    f_ref, q_ref, k_ref, v_ref, gf_ref, ga_ref):
    h = _rms(x_ref[...], g_ref[...]).astype(BF16)
    o = 0
    f_ref[...] = _dot(h, w_ref[:, o:o + FOURIER_WIDTH]).astype(BF16); o += FOURIER_WIDTH
    q = _dot(h, w_ref[:, o:o + Q_WIDTH]); o += Q_WIDTH
    q_ref[...] = _rope(q, cq_ref[...], sqa_ref[...], sqb_ref[...]).astype(BF16)
    k = _dot(h, w_ref[:, o:o + Q_WIDTH]); o += Q_WIDTH
    k_ref[...] = _rope(k, ck_ref[...], ska_ref[...], skb_ref[...]).astype(BF16)
    v_ref[...] = _dot(h, w_ref[:, o:o + ATTN_WIDTH]).astype(BF16); o += ATTN_WIDTH
    gf_ref[...] = _sigmoid(_dot(h, w_ref[:, o:o + D_MODEL])).astype(BF16); o += D_MODEL
    ga_ref[...] = _sigmoid(_dot(h, w_ref[:, o:o + D_MODEL])).astype(BF16)


def _rope_tables(seq, scale):
    half = ROPE_DIM // 2
    pos = jnp.arange(seq, dtype=F32)
    inv_freq = ROPE_THETA ** (-(jnp.arange(0, ROPE_DIM, 2, dtype=F32) / ROPE_DIM))
    ang = pos[:, None] * inv_freq[None, :]
    cos, sin = jnp.cos(ang), jnp.sin(ang)
    ones = jnp.ones((seq, QK_HEAD_DIM - ROPE_DIM), F32)
    zeros = jnp.zeros((seq, QK_HEAD_DIM - ROPE_DIM), F32)
    zh = jnp.zeros((seq, half), F32)
    c = jnp.concatenate([cos, cos, ones], axis=1)
    sa = jnp.concatenate([zh, sin, zeros], axis=1)
    sb = jnp.concatenate([-sin, zh, zeros], axis=1)
    rep = LANES // QK_HEAD_DIM
    return tuple(jnp.tile(t * scale, (1, rep)) for t in (c, sa, sb))


def _in_proj(x2d, gain, w_in, seq, tm):
    n_tok = x2d.shape[0]
    q_scale = LOG2E / math.sqrt(QK_HEAD_DIM)
    tabs = _rope_tables(seq, q_scale) + _rope_tables(seq, 1.0)
    spt = seq // tm
    tok = lambda w: pl.BlockSpec((tm, w), lambda i: (i, 0))
    tab = pl.BlockSpec((tm, LANES), lambda i: (i % spt, 0))
    widths = (FOURIER_WIDTH, Q_WIDTH, Q_WIDTH, ATTN_WIDTH, D_MODEL, D_MODEL)
    return pl.pallas_call(
        _in_proj_kernel,
        grid=(n_tok // tm,),
        in_specs=[tok(D_MODEL), _const_spec((1, D_MODEL)), _const_spec(w_in.shape)] + [tab] * 6,
        out_specs=[tok(w) for w in widths],
        out_shape=[jax.ShapeDtypeStruct((n_tok, w), BF16) for w in widths],
        compiler_params=pltpu.CompilerParams(dimension_semantics=("parallel",),
                                             vmem_limit_bytes=VMEM_LIMIT),
        name="in_proj",
    )(x2d, gain, w_in, *tabs)


def _chan_dft_kernel(f_ref, w_ref, o_ref):
    ab = _dot(f_ref[0], w_ref[...])
    o_ref[0] = ab[:, :FOURIER_WIDTH].astype(BF16)
    o_ref[1] = ab[:, FOURIER_WIDTH:].astype(BF16)


def _chan_dft_weights():
    n = FOURIER_GROUP_DIM
    idx = np.arange(n)
    ang = 2.0 * np.pi * ((idx[:, None] * idx[None, :]) % n) / n
    eye = np.eye(N_FOURIER_GROUPS)
    c = np.kron(eye, np.cos(ang)) / math.sqrt(n)
    s = np.kron(eye, np.sin(ang)) / math.sqrt(n)
    return jnp.asarray(np.concatenate([c, s], axis=1), dtype=BF16)


def _chan_dft(f3d, tm):
    b, seq, _ = f3d.shape
    return pl.pallas_call(
        _chan_dft_kernel,
        grid=(b, seq // tm),
        in_specs=[pl.BlockSpec((1, tm, FOURIER_WIDTH), lambda bi, i: (bi, i, 0)),
                  _const_spec((FOURIER_WIDTH, 2 * FOURIER_WIDTH))],
        out_specs=pl.BlockSpec((2, tm, FOURIER_WIDTH), lambda bi, i: (0, i, bi)),
        out_shape=jax.ShapeDtypeStruct((2, seq, b * FOURIER_WIDTH), BF16),
        compiler_params=pltpu.CompilerParams(dimension_semantics=("parallel", "parallel"),
                                             vmem_limit_bytes=VMEM_LIMIT),
        name="chan_dft",
    )(f3d, _chan_dft_weights())


def _seq_dft_table(seq):
    j = jnp.arange(seq, dtype=jnp.int32)
    ph = (j[:, None] * j[None, :]) % seq
    ang = ph.astype(F32) * (2.0 * math.pi / seq)
    sc = 1.0 / math.sqrt(seq)
    return jnp.concatenate([jnp.cos(ang) * sc, jnp.sin(ang) * (-sc)], axis=1).astype(BF16)


def _matmul_kernel(a_ref, b_ref, o_ref, acc_ref):
    kk = pl.program_id(2)

    @pl.when(kk == 0)
    def _():
        acc_ref[...] = jnp.zeros_like(acc_ref)

    acc_ref[...] += _dot(a_ref[...], b_ref[...])

    @pl.when(kk == pl.num_programs(2) - 1)
    def _():
        o_ref[...] = acc_ref[...].astype(o_ref.dtype)


def _matmul(a, b, tm, tn, tk):
    m, k = a.shape
    _, n = b.shape
    return pl.pallas_call(
        _matmul_kernel,
        grid=(m // tm, n // tn, k // tk),
        in_specs=[pl.BlockSpec((tm, tk), lambda i, j, kk: (i, kk)),
                  pl.BlockSpec((tk, tn), lambda i, j, kk: (kk, j))],
        out_specs=pl.BlockSpec((tm, tn), lambda i, j, kk: (i, j)),
        out_shape=jax.ShapeDtypeStruct((m, n), BF16),
        scratch_shapes=[pltpu.VMEM((tm, tn), F32)],
        compiler_params=pltpu.CompilerParams(
            dimension_semantics=("parallel", "parallel", "arbitrary"),
            vmem_limit_bytes=VMEM_LIMIT),
        name="seq_dft",
    )(a, b)


def _lambda_kernel(lam_init, q1_ref, k1_ref, q2_ref, k2_ref, o_ref):
    s1 = jnp.sum(q1_ref[...] * k1_ref[...], axis=-1, keepdims=True)
    s2 = jnp.sum(q2_ref[...] * k2_ref[...], axis=-1, keepdims=True)
    o_ref[...] = jnp.exp(s1) - jnp.exp(s2) + lam_init


def _lambda(lam_init, q1, k1, q2, k2):
    vec = lambda a: a.reshape(1, QK_HEAD_DIM).astype(F32)
    return pl.pallas_call(
        functools.partial(_lambda_kernel, lam_init),
        out_shape=jax.ShapeDtypeStruct((1, 1), F32),
        name="diff_lambda",
    )(vec(q1), vec(k1), vec(q2), vec(k2))


def _attn_kernel(tq, tk, rc, out_scale, lam_ref, q_ref, k_ref, v_ref, sub_ref, o_ref,
                 s0_ref, s1_ref, p_ref, m0_ref, m1_ref, l0_ref, l1_ref, acc_ref):
    s_refs, m_refs, l_refs = (s0_ref, s1_ref), (m0_ref, m1_ref), (l0_ref, l1_ref)
    seq = q_ref.shape[1]
    nk = seq // tk
    n_steps = (seq // tq) * nk
    assert nk >= 2 and nk & (nk - 1) == 0 and (2 * tq) % rc == 0
    nk_shift = nk.bit_length() - 1
    lam = lam_ref[0, 0]
    lane = lax.broadcasted_iota(jnp.int32, (1, V_HEAD_DIM), 1)
    comp1 = lane < QK_HEAD_DIM
    nt = (((1,), (1,)), ((), ()))

    def split(t):
        return lax.shift_right_logical(t, nk_shift), lax.bitwise_and(t, nk - 1)

    def scores(t, slot):
        qi, kj = split(t)
        q = q_ref[0, pl.ds(pl.multiple_of(qi * tq, tq), tq), :]
        zero = jnp.zeros_like(q)
        qz = jnp.concatenate([jnp.where(comp1, q, zero), jnp.where(comp1, zero, q)], axis=0)
        kt = k_ref[0, pl.ds(pl.multiple_of(kj * tk, tk), tk), :]
        s_refs[slot][...] = lax.dot_general(qz, kt, nt, preferred_element_type=F32)

    def softmax_pv(t, slot, can_be_first):
        qi, kj = split(t)
        first = kj == 0
        m_in, l_in, m_out, l_out = m_refs[slot], l_refs[slot], m_refs[1 - slot], l_refs[1 - slot]
        alphas = []
        for r in range(2 * tq // rc):
            rows = slice(r * rc, (r + 1) * rc)
            blocks = [s_refs[slot][rows, j * LANES:(j + 1) * LANES] for j in range(tk // LANES)]
            m_old, l_old = m_in[rows, :], l_in[rows, :]
            if can_be_first:
                m_old = jnp.where(first, -jnp.inf, m_old)
                l_old = jnp.where(first, 0.0, l_old)
            blk_max = functools.reduce(jnp.maximum, blocks)
            m_new = jnp.maximum(m_old, jnp.max(blk_max, axis=-1, keepdims=True))
            alpha = jnp.exp2(m_old - m_new)
            ps = [jnp.exp2(b - m_new) for b in blocks]
            blk_sum = functools.reduce(jnp.add, ps)
            l_out[rows, :] = alpha * l_old + jnp.sum(blk_sum, axis=-1, keepdims=True)
            m_out[rows, :] = m_new
            p_ref[rows, :] = jnp.concatenate(ps, axis=1).astype(BF16)
            alphas.append(alpha)
        vt = v_ref[0, pl.ds(pl.multiple_of(kj * tk, tk), tk), :]
        acc = acc_ref[...]
        if can_be_first:
            acc = jnp.where(first, 0.0, acc)
        acc_ref[...] = jnp.concatenate(alphas, axis=0) * acc + _dot(p_ref[...], vt)
        return qi, kj

    m_refs[0][...] = jnp.full(m_refs[0].shape, -jnp.inf, F32)
    l_refs[0][...] = jnp.zeros(l_refs[0].shape, F32)
    acc_ref[...] = jnp.zeros(acc_ref.shape, F32)
    scores(0, 0)

    def body(i, _):
        t = 2 * i
        scores(t + 1, 1)
        softmax_pv(t, 0, True)
        scores(jnp.minimum(t + 2, n_steps - 1), 0)
        qi, kj = softmax_pv(t + 1, 1, False)

        @pl.when(kj == nk - 1)
        def _():
            o = acc_ref[...] / l_refs[0][...]
            att = o[:tq] - lam * o[tq:]
            o_ref[0, pl.ds(pl.multiple_of(qi * tq, tq), tq), :] = _rms(
                att, sub_ref[...] * out_scale).astype(BF16)
        return 0

    lax.fori_loop(0, n_steps // 2, body, 0)


def _diff_attn(lam, q, k, v, subln, out_scale, tq, tk, rc):
    b, seq, _ = q.shape
    head = pl.BlockSpec((1, seq, V_HEAD_DIM), lambda bi, h: (bi, 0, h))
    return pl.pallas_call(
        functools.partial(_attn_kernel, tq, tk, rc, out_scale),
        grid=(b, N_HEADS),
        in_specs=[pl.BlockSpec(memory_space=pltpu.SMEM), head, head, head,
                  _const_spec((1, V_HEAD_DIM))],
        out_specs=head,
        out_shape=jax.ShapeDtypeStruct((b, seq, ATTN_WIDTH), BF16),
        scratch_shapes=[pltpu.VMEM((2 * tq, tk), F32)] * 2 + [pltpu.VMEM((2 * tq, tk), BF16)]
        + [pltpu.VMEM((2 * tq, LANES), F32)] * 5,
        compiler_params=pltpu.CompilerParams(dimension_semantics=("parallel", "parallel"),
                                             vmem_limit_bytes=VMEM_LIMIT),
        name="diff_attn",
    )(lam, q, k, v, subln)


def _merge_kernel(x_ref, y_ref, att_ref, gf_ref, ga_ref, wf_ref, wa_ref, wo_ref, g_ref, o_ref):
    fo = _dot(y_ref[...], wf_ref[...])
    ao = _dot(att_ref[...], wa_ref[...])
    merged = gf_ref[...].astype(F32) * fo + ga_ref[...].astype(F32) * ao
    o_ref[...] = x_ref[...] + _rms(_dot(merged.astype(BF16), wo_ref[...]), g_ref[...])


def _merge(x2d, y2, att2d, gf, ga, w_fourier, w_attn, w_out, gain, seq, tm):
    n_tok = x2d.shape[0]
    spt = seq // tm
    tok = lambda w: pl.BlockSpec((tm, w), lambda i: (i, 0))
    return pl.pallas_call(
        _merge_kernel,
        grid=(n_tok // tm,),
        in_specs=[tok(D_MODEL),
                  pl.BlockSpec((tm, FOURIER_WIDTH), lambda i: (i % spt, i // spt)),
                  tok(ATTN_WIDTH), tok(D_MODEL), tok(D_MODEL),
                  _const_spec(w_fourier.shape), _const_spec(w_attn.shape),
                  _const_spec(w_out.shape), _const_spec((1, D_MODEL))],
        out_specs=tok(D_MODEL),
        out_shape=jax.ShapeDtypeStruct((n_tok, D_MODEL), F32),
        compiler_params=pltpu.CompilerParams(dimension_semantics=("parallel",),
                                             vmem_limit_bytes=VMEM_LIMIT),
        name="merge",
    )(x2d, y2, att2d, gf, ga, w_fourier, w_attn, w_out, gain)


def _mlp_ple_kernel(x_ref, p_ref, gpre_ref, wup_ref, wdn_ref, gpost_ref, wple_ref, wgate_ref,
                    gple_ref, o_ref):
    x = x_ref[...]
    h = _rms(x, gpre_ref[...]).astype(BF16)
    d = jnp.zeros(x.shape, F32)
    for c in range(D_FF // D_MODEL):
        cols = slice(c * D_MODEL, (c + 1) * D_MODEL)
        u = jnp.square(jnp.maximum(_dot(h, wup_ref[:, cols]), 0.0))
        d = d + _dot(u.astype(BF16), wdn_ref[cols, :])
    x = x + _rms(d, gpost_ref[...])
    e = _dot(p_ref[...].astype(BF16), wple_ref[...]) * _sigmoid(_dot(x.astype(BF16), wgate_ref[...]))
    o_ref[...] = x + _rms(e, gple_ref[...])


def _mlp_ple(x2d, p2d, g_pre, w_up, w_down, g_post, w_ple, w_gate, g_ple, tm):
    n_tok = x2d.shape[0]
    tok = lambda w: pl.BlockSpec((tm, w), lambda i: (i, 0))
    vec = _const_spec((1, D_MODEL))
    return pl.pallas_call(
        _mlp_ple_kernel,
        grid=(n_tok // tm,),
        in_specs=[tok(D_MODEL), tok(PLE_DIM), vec, _const_spec(w_up.shape),
                  _const_spec(w_down.shape), vec, _const_spec(w_ple.shape),
                  _const_spec(w_gate.shape), vec],
        out_specs=tok(D_MODEL),
        out_shape=jax.ShapeDtypeStruct((n_tok, D_MODEL), F32),
        compiler_params=pltpu.CompilerParams(dimension_semantics=("parallel",),
                                             vmem_limit_bytes=VMEM_LIMIT),
        name="mlp_ple",
    )(x2d, p2d, g_pre, w_up, w_down, g_post, w_ple, w_gate, g_ple)


def _layer(x, p, layer_idx, prm):
    (norm_mix_pre, w_in, w_fourier, w_attn, w_out, lq1, lk1, lq2, lk2, subln,
     norm_mix_post, norm_mlp_pre, w_up, w_down, norm_mlp_post, w_ple, w_gate, norm_ple_post) = prm
    b, seq, _ = x.shape
    n_tok = b * seq
    row = lambda g: g.reshape(1, -1).astype(F32)
    x2d = x.reshape(n_tok, D_MODEL)

    f, q, k, v, gf, ga = _in_proj(x2d, row(norm_mix_pre), w_in.astype(BF16), seq, tm=512)

    ab = _chan_dft(f.reshape(b, seq, FOURIER_WIDTH), tm=512)
    y2 = _matmul(_seq_dft_table(seq), ab.reshape(2 * seq, b * FOURIER_WIDTH), 1024, 1024, 1024)

    lam_init = 0.8 - 0.6 * math.exp(-0.3 * layer_idx)
    lam = _lambda(lam_init, lq1, lk1, lq2, lk2)
    shp = (b, seq, Q_WIDTH)
    att = _diff_attn(lam, q.reshape(shp), k.reshape(shp), v.reshape(shp), row(subln),
                     1.0 - lam_init, tq=256, tk=512, rc=64)

    x1 = _merge(x2d, y2, att.reshape(n_tok, ATTN_WIDTH), gf, ga, w_fourier.astype(BF16),
                w_attn.astype(BF16), w_out.astype(BF16), row(norm_mix_post), seq, tm=512)
    out = _mlp_ple(x1, p.reshape(n_tok, PLE_DIM), row(norm_mlp_pre), w_up.astype(BF16),
                   w_down.astype(BF16), row(norm_mlp_post), w_ple.astype(BF16),
                   w_gate.astype(BF16), row(norm_ple_post), tm=256)
    return out.reshape(b, seq, D_MODEL)


def kernel(x_prompt, x_sample, p_prompt, p_sample, norm_mix_pre, w_in, w_fourier, w_attn, w_out,
           lambda_q1, lambda_k1, lambda_q2, lambda_k2, subln, norm_mix_post, norm_mlp_pre,
           w_up, w_down, norm_mlp_post, w_ple, w_ple_gate, norm_ple_post):
    y_prompt, y_sample = x_prompt, x_sample
    for i in range(w_in.shape[0]):
        prm = (norm_mix_pre[i], w_in[i], w_fourier[i], w_attn[i], w_out[i],
               lambda_q1[i], lambda_k1[i], lambda_q2[i], lambda_k2[i], subln[i],
               norm_mix_post[i], norm_mlp_pre[i], w_up[i], w_down[i], norm_mlp_post[i],
               w_ple[i], w_ple_gate[i], norm_ple_post[i])
        y_prompt = _layer(y_prompt, p_prompt[i], i, prm)
        y_sample = _layer(y_sample, p_sample[i], i, prm)
    return (y_prompt, y_sample)
```

```python
import functools
import math

import jax
import jax.numpy as jnp
import numpy as np
from jax import lax
from jax.experimental import pallas as pl
from jax.experimental.pallas import tpu as pltpu

D_MODEL = 1024
N_HEADS = 8
QK_HEAD_DIM = 64
V_HEAD_DIM = 128
Q_WIDTH = N_HEADS * 2 * QK_HEAD_DIM
ATTN_WIDTH = N_HEADS * V_HEAD_DIM
N_FOURIER_GROUPS = 4
FOURIER_GROUP_DIM = 128
FOURIER_WIDTH = N_FOURIER_GROUPS * FOURIER_GROUP_DIM
ROPE_DIM = QK_HEAD_DIM // 4
ROPE_THETA = 500000.0
D_FF = 4 * D_MODEL
PLE_DIM = 256
EPS = 1e-6

LANES = 128
VMEM_LIMIT = 56 * 1024 * 1024
F32 = jnp.float32
BF16 = jnp.bfloat16
LOG2E = 1.4426950408889634


def _rms(x, g):
    return x * lax.rsqrt(jnp.mean(x * x, axis=-1, keepdims=True) + EPS) * g


def _dot(a, b):
    return jnp.dot(a, b, preferred_element_type=F32)


def _sigmoid(x):
    return 1.0 / (1.0 + jnp.exp(-x))


def _const_spec(shape):
    nd = len(shape)
    return pl.BlockSpec(shape, lambda *_: (0,) * nd, pipeline_mode=pl.Buffered(1))


def _rope(y, c, sa, sb):
    outs = []
    for j in range(y.shape[1] // LANES):
        yc = y[:, j * LANES:(j + 1) * LANES]
        outs.append(yc * c + pltpu.roll(yc, ROPE_DIM // 2, 1) * sa
                    + pltpu.roll(yc, LANES - ROPE_DIM // 2, 1) * sb)
    return jnp.concatenate(outs, axis=1)


def _in_proj_kernel(x_ref, g_ref, w_ref, cq_ref, sqa_ref, sqb_ref, ck_ref, ska_ref, skb_ref,
                    f_ref, q_ref, k_ref, v_ref, gf_ref, ga_ref):
    h = _rms(x_ref[...], g_ref[...]).astype(BF16)
    o = 0
    f_ref[...] = _dot(h, w_ref[:, o:o + FOURIER_WIDTH]).astype(BF16); o += FOURIER_WIDTH
    q = _dot(h, w_ref[:, o:o + Q_WIDTH]); o += Q_WIDTH
    q_ref[...] = _rope(q, cq_ref[...], sqa_ref[...], sqb_ref[...]).astype(BF16)
    k = _dot(h, w_ref[:, o:o + Q_WIDTH]); o += Q_WIDTH
    k_ref[...] = _rope(k, ck_ref[...], ska_ref[...], skb_ref[...]).astype(BF16)
    v_ref[...] = _dot(h, w_ref[:, o:o + ATTN_WIDTH]).astype(BF16); o += ATTN_WIDTH
    gf_ref[...] = _sigmoid(_dot(h, w_ref[:, o:o + D_MODEL])).astype(BF16); o += D_MODEL
    ga_ref[...] = _sigmoid(_dot(h, w_ref[:, o:o + D_MODEL])).astype(BF16)


def _rope_tables(seq, scale):
    half = ROPE_DIM // 2
    pos = jnp.arange(seq, dtype=F32)
    inv_freq = ROPE_THETA ** (-(jnp.arange(0, ROPE_DIM, 2, dtype=F32) / ROPE_DIM))
    ang = pos[:, None] * inv_freq[None, :]
    cos, sin = jnp.cos(ang), jnp.sin(ang)
    ones = jnp.ones((seq, QK_HEAD_DIM - ROPE_DIM), F32)
    zeros = jnp.zeros((seq, QK_HEAD_DIM - ROPE_DIM), F32)
    zh = jnp.zeros((seq, half), F32)
    c = jnp.concatenate([cos, cos, ones], axis=1)
    sa = jnp.concatenate([zh, sin, zeros], axis=1)
    sb = jnp.concatenate([-sin, zh, zeros], axis=1)
    rep = LANES // QK_HEAD_DIM
    return tuple(jnp.tile(t * scale, (1, rep)) for t in (c, sa, sb))


def _in_proj(x2d, gain, w_in, seq, tm):
    n_tok = x2d.shape[0]
    q_scale = LOG2E / math.sqrt(QK_HEAD_DIM)
    tabs = _rope_tables(seq, q_scale) + _rope_tables(seq, 1.0)
    spt = seq // tm
    tok = lambda w: pl.BlockSpec((tm, w), lambda i: (i, 0))
    tab = pl.BlockSpec((tm, LANES), lambda i: (i % spt, 0))
    widths = (FOURIER_WIDTH, Q_WIDTH, Q_WIDTH, ATTN_WIDTH, D_MODEL, D_MODEL)
    return pl.pallas_call(
        _in_proj_kernel,
        grid=(n_tok // tm,),
        in_specs=[tok(D_MODEL), _const_spec((1, D_MODEL)), _const_spec(w_in.shape)] + [tab] * 6,
        out_specs=[tok(w) for w in widths],
        out_shape=[jax.ShapeDtypeStruct((n_tok, w), BF16) for w in widths],
        compiler_params=pltpu.CompilerParams(dimension_semantics=("parallel",),
                                             vmem_limit_bytes=VMEM_LIMIT),
        name="in_proj",
    )(x2d, gain, w_in, *tabs)


def _chan_dft_kernel(f_ref, w_ref, o_ref):
    ab = _dot(f_ref[0], w_ref[...])
    o_ref[0] = ab[:, :FOURIER_WIDTH].astype(BF16)
    o_ref[1] = ab[:, FOURIER_WIDTH:].astype(BF16)


def _chan_dft_weights():
    n = FOURIER_GROUP_DIM
    idx = np.arange(n)
    ang = 2.0 * np.pi * ((idx[:, None] * idx[None, :]) % n) / n
    eye = np.eye(N_FOURIER_GROUPS)
    c = np.kron(eye, np.cos(ang)) / math.sqrt(n)
    s = np.kron(eye, np.sin(ang)) / math.sqrt(n)
    return jnp.asarray(np.concatenate([c, s], axis=1), dtype=BF16)


def _chan_dft(f3d, tm):
    b, seq, _ = f3d.shape
    return pl.pallas_call(
        _chan_dft_kernel,
        grid=(b, seq // tm),
        in_specs=[pl.BlockSpec((1, tm, FOURIER_WIDTH), lambda bi, i: (bi, i, 0)),
                  _const_spec((FOURIER_WIDTH, 2 * FOURIER_WIDTH))],
        out_specs=pl.BlockSpec((2, tm, FOURIER_WIDTH), lambda bi, i: (0, i, bi)),
        out_shape=jax.ShapeDtypeStruct((2, seq, b * FOURIER_WIDTH), BF16),
        compiler_params=pltpu.CompilerParams(dimension_semantics=("parallel", "parallel"),
                                             vmem_limit_bytes=VMEM_LIMIT),
        name="chan_dft",
    )(f3d, _chan_dft_weights())


def _seq_dft_table(seq):
    j = jnp.arange(seq, dtype=jnp.int32)
    ph = (j[:, None] * j[None, :]) % seq
    ang = ph.astype(F32) * (2.0 * math.pi / seq)
    sc = 1.0 / math.sqrt(seq)
    return jnp.concatenate([jnp.cos(ang) * sc, jnp.sin(ang) * (-sc)], axis=1).astype(BF16)


def _matmul_kernel(a_ref, b_ref, o_ref, acc_ref):
    kk = pl.program_id(2)

    @pl.when(kk == 0)
    def _():
        acc_ref[...] = jnp.zeros_like(acc_ref)

    acc_ref[...] += _dot(a_ref[...], b_ref[...])

    @pl.when(kk == pl.num_programs(2) - 1)
    def _():
        o_ref[...] = acc_ref[...].astype(o_ref.dtype)


def _matmul(a, b, tm, tn, tk):
    m, k = a.shape
    _, n = b.shape
    return pl.pallas_call(
        _matmul_kernel,
        grid=(m // tm, n // tn, k // tk),
        in_specs=[pl.BlockSpec((tm, tk), lambda i, j, kk: (i, kk)),
                  pl.BlockSpec((tk, tn), lambda i, j, kk: (kk, j))],
        out_specs=pl.BlockSpec((tm, tn), lambda i, j, kk: (i, j)),
        out_shape=jax.ShapeDtypeStruct((m, n), BF16),
        scratch_shapes=[pltpu.VMEM((tm, tn), F32)],
        compiler_params=pltpu.CompilerParams(
            dimension_semantics=("parallel", "parallel", "arbitrary"),
            vmem_limit_bytes=VMEM_LIMIT),
        name="seq_dft",
    )(a, b)


def _lambda_kernel(lam_init, q1_ref, k1_ref, q2_ref, k2_ref, o_ref):
    s1 = jnp.sum(q1_ref[...] * k1_ref[...], axis=-1, keepdims=True)
    s2 = jnp.sum(q2_ref[...] * k2_ref[...], axis=-1, keepdims=True)
    o_ref[...] = jnp.exp(s1) - jnp.exp(s2) + lam_init


def _lambda(lam_init, q1, k1, q2, k2):
    vec = lambda a: a.reshape(1, QK_HEAD_DIM).astype(F32)
    return pl.pallas_call(
        functools.partial(_lambda_kernel, lam_init),
        out_shape=jax.ShapeDtypeStruct((1, 1), F32),
        name="diff_lambda",
    )(vec(q1), vec(k1), vec(q2), vec(k2))


def _attn_kernel(tq, tk, unroll, fin_group, out_scale, lam_ref, q_ref, k_ref, v_ref, sub_ref, o_ref,
                 s0_ref, s1_ref, p0_ref, p1_ref, a0_ref, a1_ref, m0_ref, m1_ref, acc_ref):
    s_refs, p_refs, a_refs, m_refs = (s0_ref, s1_ref), (p0_ref, p1_ref), (a0_ref, a1_ref), (m0_ref, m1_ref)
    seq = q_ref.shape[1]
    nq, nk = seq // tq, seq // tk
    n_steps = nq * nk
    assert nk >= 2 and nk & (nk - 1) == 0 and nq % fin_group == 0
    assert unroll % 2 == 0 and n_steps % unroll == 0
    nk_shift = nk.bit_length() - 1
    lam = lam_ref[0, 0]
    lane = lax.broadcasted_iota(jnp.int32, (1, V_HEAD_DIM), 1)
    comp1 = lane < QK_HEAD_DIM
    nt = (((1,), (1,)), ((), ()))
    ones = jnp.ones((tk, V_HEAD_DIM), BF16)

    def split(t):
        return lax.shift_right_logical(t, nk_shift), lax.bitwise_and(t, nk - 1)

    def scores(t, slot):
        qi, kj = split(t)
        q = q_ref[0, pl.ds(pl.multiple_of(qi * tq, tq), tq), :]
        zero = jnp.zeros_like(q)
        qz = jnp.concatenate([jnp.where(comp1, q, zero), jnp.where(comp1, zero, q)], axis=0)
        kt = k_ref[0, pl.ds(pl.multiple_of(kj * tk, tk), tk), :]
        s_refs[slot][...] = lax.dot_general(qz, kt, nt, preferred_element_type=F32)

    def softmax(t, slot, can_be_first):
        _, kj = split(t)
        s_ref, m_in, m_out = s_refs[slot], m_refs[slot], m_refs[1 - slot]
        for g in range(2 * tq // 8):
            rows = slice(g * 8, (g + 1) * 8)
            blocks = [s_ref[rows, j * LANES:(j + 1) * LANES] for j in range(tk // LANES)]
            m_old = m_in[rows, :]
            if can_be_first:
                m_old = jnp.where(kj == 0, -jnp.inf, m_old)
            blk_max = functools.reduce(jnp.maximum, blocks)
            m_new = jnp.maximum(m_old, jnp.max(blk_max, axis=-1, keepdims=True))
            m_out[rows, :] = m_new
            a_refs[slot][rows, :] = jnp.exp2(m_old - m_new)
            p_refs[slot][rows, :] = jnp.concatenate(
                [jnp.exp2(b - m_new) for b in blocks], axis=1).astype(BF16)

    def pv(u, slot, tile):
        _, kj = split(u)
        vt = v_ref[0, pl.ds(pl.multiple_of(kj * tk, tk), tk), :]
        v1 = jnp.concatenate([vt, ones], axis=1)
        alpha = a_refs[slot][...]
        alpha2 = jnp.concatenate([alpha, alpha], axis=1)
        acc_ref[tile] = alpha2 * acc_ref[tile] + _dot(p_refs[slot][...], v1)

    @pl.when((pl.program_id(0) == 0) & (pl.program_id(1) == 0))
    def _():
        acc_ref[...] = jnp.zeros(acc_ref.shape, F32)

    m0_ref[...] = jnp.full(m0_ref.shape, -jnp.inf, F32)
    scores(0, 0)

    def body(i, _):
        t = unroll * i
        for j in range(unroll):
            scores(jnp.minimum(t + j + 1, n_steps - 1), (j + 1) % 2)
            softmax(t + j, j % 2, j % math.gcd(unroll, nk) == 0)
            pv(t + j, j % 2, split(t + j)[0])
        return 0

    lax.fori_loop(0, n_steps // unroll, body, 0)

    sub = sub_ref[...] * out_scale

    def finish(i, _):
        for j in range(fin_group):
            tile = i * fin_group + j
            a = acc_ref[tile]
            acc_ref[tile] = jnp.zeros(a.shape, F32)
            o = a[:, :V_HEAD_DIM] / a[:, V_HEAD_DIM:]
            att = o[:tq] - lam * o[tq:]
            o_ref[0, pl.ds(pl.multiple_of(tile * tq, tq), tq), :] = _rms(att, sub).astype(BF16)
        return 0

    lax.fori_loop(0, nq // fin_group, finish, 0)


def _diff_attn(lam, q, k, v, subln, out_scale, tq, tk, unroll, fin_group):
    b, seq, _ = q.shape
    head = pl.BlockSpec((1, seq, V_HEAD_DIM), lambda bi, h: (bi, 0, h))
    return pl.pallas_call(
        functools.partial(_attn_kernel, tq, tk, unroll, fin_group, out_scale),
        grid=(b, N_HEADS),
        in_specs=[pl.BlockSpec(memory_space=pltpu.SMEM), head, head, head,
                  _const_spec((1, V_HEAD_DIM))],
        out_specs=head,
        out_shape=jax.ShapeDtypeStruct((b, seq, ATTN_WIDTH), BF16),
        scratch_shapes=[pltpu.VMEM((2 * tq, tk), F32)] * 2 + [pltpu.VMEM((2 * tq, tk), BF16)] * 2
        + [pltpu.VMEM((2 * tq, LANES), F32)] * 4
        + [pltpu.VMEM((seq // tq, 2 * tq, 2 * V_HEAD_DIM), F32)],
        compiler_params=pltpu.CompilerParams(dimension_semantics=("arbitrary", "arbitrary"),
                                             vmem_limit_bytes=VMEM_LIMIT),
        name="diff_attn",
    )(lam, q, k, v, subln)


def _merge_kernel(x_ref, y_ref, att_ref, gf_ref, ga_ref, wf_ref, wa_ref, wo_ref, g_ref, o_ref):
    fo = _dot(y_ref[...], wf_ref[...])
    ao = _dot(att_ref[...], wa_ref[...])
    merged = gf_ref[...].astype(F32) * fo + ga_ref[...].astype(F32) * ao
    o_ref[...] = x_ref[...] + _rms(_dot(merged.astype(BF16), wo_ref[...]), g_ref[...])


def _merge(x2d, y2, att2d, gf, ga, w_fourier, w_attn, w_out, gain, seq, tm):
    n_tok = x2d.shape[0]
    spt = seq // tm
    tok = lambda w: pl.BlockSpec((tm, w), lambda i: (i, 0))
    return pl.pallas_call(
        _merge_kernel,
        grid=(n_tok // tm,),
        in_specs=[tok(D_MODEL),
                  pl.BlockSpec((tm, FOURIER_WIDTH), lambda i: (i % spt, i // spt)),
                  tok(ATTN_WIDTH), tok(D_MODEL), tok(D_MODEL),
                  _const_spec(w_fourier.shape), _const_spec(w_attn.shape),
                  _const_spec(w_out.shape), _const_spec((1, D_MODEL))],
        out_specs=tok(D_MODEL),
        out_shape=jax.ShapeDtypeStruct((n_tok, D_MODEL), F32),
        compiler_params=pltpu.CompilerParams(dimension_semantics=("parallel",),
                                             vmem_limit_bytes=VMEM_LIMIT),
        name="merge",
    )(x2d, y2, att2d, gf, ga, w_fourier, w_attn, w_out, gain)


def _mlp_ple_kernel(x_ref, p_ref, gpre_ref, wup_ref, wdn_ref, gpost_ref, wple_ref, wgate_ref,
                    gple_ref, o_ref):
    x = x_ref[...]
    h = _rms(x, gpre_ref[...]).astype(BF16)
    d = jnp.zeros(x.shape, F32)
    for c in range(D_FF // D_MODEL):
        cols = slice(c * D_MODEL, (c + 1) * D_MODEL)
        u = jnp.square(jnp.maximum(_dot(h, wup_ref[:, cols]), 0.0))
        d = d + _dot(u.astype(BF16), wdn_ref[cols, :])
    x = x + _rms(d, gpost_ref[...])
    e = _dot(p_ref[...].astype(BF16), wple_ref[...]) * _sigmoid(_dot(x.astype(BF16), wgate_ref[...]))
    o_ref[...] = x + _rms(e, gple_ref[...])


def _mlp_ple(x2d, p2d, g_pre, w_up, w_down, g_post, w_ple, w_gate, g_ple, tm):
    n_tok = x2d.shape[0]
    tok = lambda w: pl.BlockSpec((tm, w), lambda i: (i, 0))
    vec = _const_spec((1, D_MODEL))
    return pl.pallas_call(
        _mlp_ple_kernel,
        grid=(n_tok // tm,),
        in_specs=[tok(D_MODEL), tok(PLE_DIM), vec, _const_spec(w_up.shape),
                  _const_spec(w_down.shape), vec, _const_spec(w_ple.shape),
                  _const_spec(w_gate.shape), vec],
        out_specs=tok(D_MODEL),
        out_shape=jax.ShapeDtypeStruct((n_tok, D_MODEL), F32),
        compiler_params=pltpu.CompilerParams(dimension_semantics=("parallel",),
                                             vmem_limit_bytes=VMEM_LIMIT),
        name="mlp_ple",
    )(x2d, p2d, g_pre, w_up, w_down, g_post, w_ple, w_gate, g_ple)


def _layer(x, p, layer_idx, prm):
    (norm_mix_pre, w_in, w_fourier, w_attn, w_out, lq1, lk1, lq2, lk2, subln,
     norm_mix_post, norm_mlp_pre, w_up, w_down, norm_mlp_post, w_ple, w_gate, norm_ple_post) = prm
    b, seq, _ = x.shape
    n_tok = b * seq
    row = lambda g: g.reshape(1, -1).astype(F32)
    x2d = x.reshape(n_tok, D_MODEL)

    f, q, k, v, gf, ga = _in_proj(x2d, row(norm_mix_pre), w_in.astype(BF16), seq, tm=512)

    ab = _chan_dft(f.reshape(b, seq, FOURIER_WIDTH), tm=512)
    y2 = _matmul(_seq_dft_table(seq), ab.reshape(2 * seq, b * FOURIER_WIDTH), 1024, 1024, 1024)

    lam_init = 0.8 - 0.6 * math.exp(-0.3 * layer_idx)
    lam = _lambda(lam_init, lq1, lk1, lq2, lk2)
    shp = (b, seq, Q_WIDTH)
    att = _diff_attn(lam, q.reshape(shp), k.reshape(shp), v.reshape(shp), row(subln),
                     1.0 - lam_init, tq=256, tk=1024, unroll=8, fin_group=4)

    x1 = _merge(x2d, y2, att.reshape(n_tok, ATTN_WIDTH), gf, ga, w_fourier.astype(BF16),
                w_attn.astype(BF16), w_out.astype(BF16), row(norm_mix_post), seq, tm=512)
    out = _mlp_ple(x1, p.reshape(n_tok, PLE_DIM), row(norm_mlp_pre), w_up.astype(BF16),
                   w_down.astype(BF16), row(norm_mlp_post), w_ple.astype(BF16),
                   w_gate.astype(BF16), row(norm_ple_post), tm=256)
    return out.reshape(b, seq, D_MODEL)


def kernel(x_prompt, x_sample, p_prompt, p_sample, norm_mix_pre, w_in, w_fourier, w_attn, w_out,
           lambda_q1, lambda_k1, lambda_q2, lambda_k2, subln, norm_mix_post, norm_mlp_pre,
           w_up, w_down, norm_mlp_post, w_ple, w_ple_gate, norm_ple_post):
    y_prompt, y_sample = x_prompt, x_sample
    for i in range(w_in.shape[0]):
        prm = (norm_mix_pre[i], w_in[i], w_fourier[i], w_attn[i], w_out[i],
               lambda_q1[i], lambda_k1[i], lambda_q2[i], lambda_k2[i], subln[i],
               norm_mix_post[i], norm_mlp_pre[i], w_up[i], w_down[i], norm_mlp_post[i],
               w_ple[i], w_ple_gate[i], norm_ple_post[i])
        y_prompt = _layer(y_prompt, p_prompt[i], i, prm)
        y_sample = _layer(y_sample, p_sample[i], i, prm)
    return (y_prompt, y_sample)
```

```python
import functools
import math

import jax
import jax.numpy as jnp
import numpy as np
from jax import lax
from jax.experimental import pallas as pl
from jax.experimental.pallas import tpu as pltpu

D_MODEL = 1024
N_HEADS = 8
QK_HEAD_DIM = 64
V_HEAD_DIM = 128
Q_WIDTH = N_HEADS * 2 * QK_HEAD_DIM
ATTN_WIDTH = N_HEADS * V_HEAD_DIM
N_FOURIER_GROUPS = 4
FOURIER_GROUP_DIM = 128
FOURIER_WIDTH = N_FOURIER_GROUPS * FOURIER_GROUP_DIM
ROPE_DIM = QK_HEAD_DIM // 4
ROPE_THETA = 500000.0
D_FF = 4 * D_MODEL
PLE_DIM = 256
EPS = 1e-6

LANES = 128
VMEM_LIMIT = 56 * 1024 * 1024
F32 = jnp.float32
BF16 = jnp.bfloat16
LOG2E = 1.4426950408889634


def _rms(x, g):
    return x * lax.rsqrt(jnp.mean(x * x, axis=-1, keepdims=True) + EPS) * g


def _dot(a, b):
    return jnp.dot(a, b, preferred_element_type=F32)


def _sigmoid(x):
    return 1.0 / (1.0 + jnp.exp(-x))


def _const_spec(shape):
    nd = len(shape)
    return pl.BlockSpec(shape, lambda *_: (0,) * nd, pipeline_mode=pl.Buffered(1))


def _rope(y, c, sa, sb):
    outs = []
    for j in range(y.shape[1] // LANES):
        yc = y[:, j * LANES:(j + 1) * LANES]
        outs.append(yc * c + pltpu.roll(yc, ROPE_DIM // 2, 1) * sa
                    + pltpu.roll(yc, LANES - ROPE_DIM // 2, 1) * sb)
    return jnp.concatenate(outs, axis=1)


def _in_proj_kernel(x_ref, g_ref, w_ref, cq_ref, sqa_ref, sqb_ref, ck_ref, ska_ref, skb_ref,
                    fa_ref, fb_ref, q_ref, k_ref, v_ref, gf_ref, ga_ref):
    h = _rms(x_ref[...], g_ref[...]).astype(BF16)
    o = 0
    fa_ref[...] = _dot(h, w_ref[:, o:o + FOURIER_WIDTH]).astype(BF16); o += FOURIER_WIDTH
    fb_ref[...] = _dot(h, w_ref[:, o:o + FOURIER_WIDTH]).astype(BF16); o += FOURIER_WIDTH
    q = _dot(h, w_ref[:, o:o + Q_WIDTH]); o += Q_WIDTH
    q_ref[...] = _rope(q, cq_ref[...], sqa_ref[...], sqb_ref[...]).astype(BF16)
    k = _dot(h, w_ref[:, o:o + Q_WIDTH]); o += Q_WIDTH
    k_ref[...] = _rope(k, ck_ref[...], ska_ref[...], skb_ref[...]).astype(BF16)
    v_ref[...] = _dot(h, w_ref[:, o:o + ATTN_WIDTH]).astype(BF16); o += ATTN_WIDTH
    gf_ref[...] = _sigmoid(_dot(h, w_ref[:, o:o + D_MODEL])).astype(BF16); o += D_MODEL
    ga_ref[...] = _sigmoid(_dot(h, w_ref[:, o:o + D_MODEL])).astype(BF16)


def _rope_tables(seq, scale):
    half = ROPE_DIM // 2
    pos = jnp.arange(seq, dtype=F32)
    inv_freq = ROPE_THETA ** (-(jnp.arange(0, ROPE_DIM, 2, dtype=F32) / ROPE_DIM))
    ang = pos[:, None] * inv_freq[None, :]
    cos, sin = jnp.cos(ang), jnp.sin(ang)
    ones = jnp.ones((seq, QK_HEAD_DIM - ROPE_DIM), F32)
    zeros = jnp.zeros((seq, QK_HEAD_DIM - ROPE_DIM), F32)
    zh = jnp.zeros((seq, half), F32)
    c = jnp.concatenate([cos, cos, ones], axis=1)
    sa = jnp.concatenate([zh, sin, zeros], axis=1)
    sb = jnp.concatenate([-sin, zh, zeros], axis=1)
    rep = LANES // QK_HEAD_DIM
    return tuple(jnp.tile(t * scale, (1, rep)) for t in (c, sa, sb))


def _in_proj(x2d, gain, w_in, seq, tm):
    n_tok = x2d.shape[0]
    q_scale = LOG2E / math.sqrt(QK_HEAD_DIM)
    tabs = _rope_tables(seq, q_scale) + _rope_tables(seq, 1.0)
    spt = seq // tm
    tok = lambda w: pl.BlockSpec((tm, w), lambda i: (i, 0))
    tab = pl.BlockSpec((tm, LANES), lambda i: (i % spt, 0))
    widths = (FOURIER_WIDTH, FOURIER_WIDTH, Q_WIDTH, Q_WIDTH, ATTN_WIDTH, D_MODEL, D_MODEL)
    return pl.pallas_call(
        _in_proj_kernel,
        grid=(n_tok // tm,),
        in_specs=[tok(D_MODEL), _const_spec((1, D_MODEL)), _const_spec(w_in.shape)] + [tab] * 6,
        out_specs=[tok(w) for w in widths],
        out_shape=[jax.ShapeDtypeStruct((n_tok, w), BF16) for w in widths],
        compiler_params=pltpu.CompilerParams(dimension_semantics=("parallel",),
                                             vmem_limit_bytes=VMEM_LIMIT),
        name="in_proj",
    )(x2d, gain, w_in, *tabs)


def _chan_dft_weights():
    n = FOURIER_GROUP_DIM
    idx = np.arange(n)
    ang = 2.0 * np.pi * ((idx[:, None] * idx[None, :]) % n) / n
    eye = np.eye(N_FOURIER_GROUPS)
    c = np.kron(eye, np.cos(ang)) / math.sqrt(n)
    s = np.kron(eye, np.sin(ang)) / math.sqrt(n)
    return jnp.asarray(np.concatenate([c, s], axis=1), dtype=BF16)


def _fold_kernel(a_ref, b_ref, o_ref):
    o_ref[...] = _dot(a_ref[...], b_ref[...]).astype(BF16)


def _fold_chan_dft(w_f):
    return pl.pallas_call(
        _fold_kernel,
        out_shape=jax.ShapeDtypeStruct((w_f.shape[0], 2 * FOURIER_WIDTH), BF16),
        compiler_params=pltpu.CompilerParams(vmem_limit_bytes=VMEM_LIMIT),
        name="fold_chan_dft",
    )(w_f, _chan_dft_weights())


SEQ_DFT_N2 = 128


def _dft_stage1_kernel(a_ref, b_ref, m_ref, tc_ref, ts_ref, vr_ref, vi_ref):
    n1 = a_ref.shape[1]
    u = _dot(m_ref[...], jnp.concatenate([a_ref[0], b_ref[0]], axis=0))
    ur, ui = u[:n1], u[n1:]
    tc, ts = tc_ref[...], ts_ref[...]
    vr_ref[0] = (ur * tc - ui * ts).astype(BF16)
    vi_ref[0] = (ur * ts + ui * tc).astype(BF16)


def _dft_stage2_kernel(vr_ref, vi_ref, m_ref, y_ref):
    for i in range(vr_ref.shape[1]):
        v = jnp.concatenate([vr_ref[0, i], vi_ref[0, i]], axis=0)
        y_ref[0, :, i * FOURIER_WIDTH:(i + 1) * FOURIER_WIDTH] = _dot(m_ref[...], v).astype(BF16)


def _seq_dft(a, b, bsz, seq, tcols, group):
    n2 = SEQ_DFT_N2
    n1 = seq // n2
    cols = n2 * FOURIER_WIDTH
    j1 = np.arange(n1)
    ang1 = 2.0 * np.pi * ((j1[:, None] * j1[None, :]) % n1) / n1
    c1, s1 = np.cos(ang1), np.sin(ang1)
    m1 = jnp.asarray(np.block([[c1, -s1], [s1, c1]]) / math.sqrt(seq), dtype=BF16)
    j2 = np.arange(n2)
    ang2 = 2.0 * np.pi * ((j2[:, None] * j2[None, :]) % n2) / n2
    m3 = jnp.asarray(np.concatenate([np.cos(ang2), -np.sin(ang2)], axis=1), dtype=BF16)
    angt = (jnp.arange(n1, dtype=F32)[:, None] * jnp.arange(n2, dtype=F32)[None, :]) * (2.0 * math.pi / seq)
    tcos = jnp.repeat(jnp.cos(angt), FOURIER_WIDTH, axis=1)
    tsin = jnp.repeat(jnp.sin(angt), FOURIER_WIDTH, axis=1)

    blk = pl.BlockSpec((1, n1, tcols), lambda c, bi: (bi, 0, c))
    twd = pl.BlockSpec((n1, tcols), lambda c, bi: (0, c))
    vr, vi = pl.pallas_call(
        _dft_stage1_kernel,
        grid=(cols // tcols, bsz),
        in_specs=[blk, blk, _const_spec(m1.shape), twd, twd],
        out_specs=[blk, blk],
        out_shape=[jax.ShapeDtypeStruct((bsz, n1, cols), BF16)] * 2,
        compiler_params=pltpu.CompilerParams(dimension_semantics=("parallel", "parallel"),
                                             vmem_limit_bytes=VMEM_LIMIT),
        name="seq_dft_stage1",
    )(a.reshape(bsz, n1, cols), b.reshape(bsz, n1, cols), m1, tcos, tsin)

    vblk = pl.BlockSpec((1, group, n2, FOURIER_WIDTH), lambda bi, g: (bi, g, 0, 0))
    y = pl.pallas_call(
        _dft_stage2_kernel,
        grid=(bsz, n1 // group),
        in_specs=[vblk, vblk, _const_spec(m3.shape)],
        out_specs=pl.BlockSpec((1, n2, group * FOURIER_WIDTH), lambda bi, g: (bi, 0, g)),
        out_shape=jax.ShapeDtypeStruct((bsz, n2, n1 * FOURIER_WIDTH), BF16),
        compiler_params=pltpu.CompilerParams(dimension_semantics=("parallel", "parallel"),
                                             vmem_limit_bytes=VMEM_LIMIT),
        name="seq_dft_stage2",
    )(vr.reshape(bsz, n1, n2, FOURIER_WIDTH), vi.reshape(bsz, n1, n2, FOURIER_WIDTH), m3)
    return y.reshape(bsz * seq, FOURIER_WIDTH)


def _lambda_kernel(lam_init, q1_ref, k1_ref, q2_ref, k2_ref, o_ref):
    s1 = jnp.sum(q1_ref[...] * k1_ref[...], axis=-1, keepdims=True)
    s2 = jnp.sum(q2_ref[...] * k2_ref[...], axis=-1, keepdims=True)
    o_ref[...] = jnp.exp(s1) - jnp.exp(s2) + lam_init


def _lambda(lam_init, q1, k1, q2, k2):
    vec = lambda a: a.reshape(1, QK_HEAD_DIM).astype(F32)
    return pl.pallas_call(
        functools.partial(_lambda_kernel, lam_init),
        out_shape=jax.ShapeDtypeStruct((1, 1), F32),
        name="diff_lambda",
    )(vec(q1), vec(k1), vec(q2), vec(k2))


def _attn_kernel(tq, tk, unroll, fin_group, out_scale, lam_ref, q_ref, k_ref, v_ref, sub_ref, o_ref,
                 s0_ref, s1_ref, p0_ref, p1_ref, a0_ref, a1_ref, m0_ref, m1_ref, acc_ref):
    s_refs, p_refs, a_refs, m_refs = (s0_ref, s1_ref), (p0_ref, p1_ref), (a0_ref, a1_ref), (m0_ref, m1_ref)
    seq = q_ref.shape[1]
    nq, nk = seq // tq, seq // tk
    n_steps = nq * nk
    assert nk >= 2 and nk & (nk - 1) == 0 and nq % fin_group == 0
    assert unroll % 2 == 0 and n_steps % unroll == 0
    nk_shift = nk.bit_length() - 1
    lam = lam_ref[0, 0]
    lane = lax.broadcasted_iota(jnp.int32, (1, V_HEAD_DIM), 1)
    comp1 = lane < QK_HEAD_DIM
    nt = (((1,), (1,)), ((), ()))
    ones = jnp.ones((tk, V_HEAD_DIM), BF16)

    def split(t):
        return lax.shift_right_logical(t, nk_shift), lax.bitwise_and(t, nk - 1)

    def scores(t, slot):
        qi, kj = split(t)
        q = q_ref[0, pl.ds(pl.multiple_of(qi * tq, tq), tq), :]
        zero = jnp.zeros_like(q)
        qz = jnp.concatenate([jnp.where(comp1, q, zero), jnp.where(comp1, zero, q)], axis=0)
        kt = k_ref[0, pl.ds(pl.multiple_of(kj * tk, tk), tk), :]
        s_refs[slot][...] = lax.dot_general(qz, kt, nt, preferred_element_type=F32)

    def softmax(t, slot, can_be_first):
        _, kj = split(t)
        s_ref, m_in, m_out = s_refs[slot], m_refs[slot], m_refs[1 - slot]
        for g in range(2 * tq // 8):
            rows = slice(g * 8, (g + 1) * 8)
            blocks = [s_ref[rows, j * LANES:(j + 1) * LANES] for j in range(tk // LANES)]
            m_old = m_in[rows, :]
            if can_be_first:
                m_old = jnp.where(kj == 0, -jnp.inf, m_old)
            blk_max = functools.reduce(jnp.maximum, blocks)
            m_new = jnp.maximum(m_old, jnp.max(blk_max, axis=-1, keepdims=True))
            m_out[rows, :] = m_new
            a_refs[slot][rows, :] = jnp.exp2(m_old - m_new)
            p_refs[slot][rows, :] = jnp.concatenate(
                [jnp.exp2(b - m_new) for b in blocks], axis=1).astype(BF16)

    def pv(u, slot, tile):
        _, kj = split(u)
        vt = v_ref[0, pl.ds(pl.multiple_of(kj * tk, tk), tk), :]
        v1 = jnp.concatenate([vt, ones], axis=1)
        alpha = a_refs[slot][...]
        alpha2 = jnp.concatenate([alpha, alpha], axis=1)
        acc_ref[tile] = alpha2 * acc_ref[tile] + _dot(p_refs[slot][...], v1)

    @pl.when((pl.program_id(0) == 0) & (pl.program_id(1) == 0))
    def _():
        acc_ref[...] = jnp.zeros(acc_ref.shape, F32)

    m0_ref[...] = jnp.full(m0_ref.shape, -jnp.inf, F32)
    scores(0, 0)

    def body(i, _):
        t = unroll * i
        for j in range(unroll):
            scores(jnp.minimum(t + j + 1, n_steps - 1), (j + 1) % 2)
            softmax(t + j, j % 2, j % math.gcd(unroll, nk) == 0)
            pv(t + j, j % 2, split(t + j)[0])
        return 0

    lax.fori_loop(0, n_steps // unroll, body, 0)

    sub = sub_ref[...] * out_scale

    def finish(i, _):
        for j in range(fin_group):
            tile = i * fin_group + j
            a = acc_ref[tile]
            acc_ref[tile] = jnp.zeros(a.shape, F32)
            o = a[:, :V_HEAD_DIM] / a[:, V_HEAD_DIM:]
            att = o[:tq] - lam * o[tq:]
            o_ref[0, pl.ds(pl.multiple_of(tile * tq, tq), tq), :] = _rms(att, sub).astype(BF16)
        return 0

    lax.fori_loop(0, nq // fin_group, finish, 0)


def _diff_attn(lam, q, k, v, subln, out_scale, tq, tk, unroll, fin_group):
    b, seq, _ = q.shape
    head = pl.BlockSpec((1, seq, V_HEAD_DIM), lambda bi, h: (bi, 0, h))
    return pl.pallas_call(
        functools.partial(_attn_kernel, tq, tk, unroll, fin_group, out_scale),
        grid=(b, N_HEADS),
        in_specs=[pl.BlockSpec(memory_space=pltpu.SMEM), head, head, head,
                  _const_spec((1, V_HEAD_DIM))],
        out_specs=head,
        out_shape=jax.ShapeDtypeStruct((b, seq, ATTN_WIDTH), BF16),
        scratch_shapes=[pltpu.VMEM((2 * tq, tk), F32)] * 2 + [pltpu.VMEM((2 * tq, tk), BF16)] * 2
        + [pltpu.VMEM((2 * tq, LANES), F32)] * 4
        + [pltpu.VMEM((seq // tq, 2 * tq, 2 * V_HEAD_DIM), F32)],
        compiler_params=pltpu.CompilerParams(dimension_semantics=("arbitrary", "arbitrary"),
                                             vmem_limit_bytes=VMEM_LIMIT),
        name="diff_attn",
    )(lam, q, k, v, subln)


def _merge_kernel(x_ref, y_ref, att_ref, gf_ref, ga_ref, wf_ref, wa_ref, wo_ref, g_ref, o_ref):
    fo = _dot(y_ref[...], wf_ref[...])
    ao = _dot(att_ref[...], wa_ref[...])
    merged = gf_ref[...].astype(F32) * fo + ga_ref[...].astype(F32) * ao
    o_ref[...] = x_ref[...] + _rms(_dot(merged.astype(BF16), wo_ref[...]), g_ref[...])


def _merge(x2d, y2, att2d, gf, ga, w_fourier, w_attn, w_out, gain, tm):
    n_tok = x2d.shape[0]
    tok = lambda w: pl.BlockSpec((tm, w), lambda i: (i, 0))
    return pl.pallas_call(
        _merge_kernel,
        grid=(n_tok // tm,),
        in_specs=[tok(D_MODEL), tok(FOURIER_WIDTH), tok(ATTN_WIDTH), tok(D_MODEL), tok(D_MODEL),
                  _const_spec(w_fourier.shape), _const_spec(w_attn.shape),
                  _const_spec(w_out.shape), _const_spec((1, D_MODEL))],
        out_specs=tok(D_MODEL),
        out_shape=jax.ShapeDtypeStruct((n_tok, D_MODEL), F32),
        compiler_params=pltpu.CompilerParams(dimension_semantics=("parallel",),
                                             vmem_limit_bytes=VMEM_LIMIT),
        name="merge",
    )(x2d, y2, att2d, gf, ga, w_fourier, w_attn, w_out, gain)


def _mlp_ple_kernel(x_ref, p_ref, gpre_ref, wup_ref, wdn_ref, gpost_ref, wple_ref, wgate_ref,
                    gple_ref, o_ref):
    x = x_ref[...]
    h = _rms(x, gpre_ref[...]).astype(BF16)
    d = jnp.zeros(x.shape, F32)
    for c in range(D_FF // D_MODEL):
        cols = slice(c * D_MODEL, (c + 1) * D_MODEL)
        u = jnp.square(jnp.maximum(_dot(h, wup_ref[:, cols]), 0.0))
        d = d + _dot(u.astype(BF16), wdn_ref[cols, :])
    x = x + _rms(d, gpost_ref[...])
    e = _dot(p_ref[...].astype(BF16), wple_ref[...]) * _sigmoid(_dot(x.astype(BF16), wgate_ref[...]))
    o_ref[...] = x + _rms(e, gple_ref[...])


def _mlp_ple(x2d, p2d, g_pre, w_up, w_down, g_post, w_ple, w_gate, g_ple, tm):
    n_tok = x2d.shape[0]
    tok = lambda w: pl.BlockSpec((tm, w), lambda i: (i, 0))
    vec = _const_spec((1, D_MODEL))
    return pl.pallas_call(
        _mlp_ple_kernel,
        grid=(n_tok // tm,),
        in_specs=[tok(D_MODEL), tok(PLE_DIM), vec, _const_spec(w_up.shape),
                  _const_spec(w_down.shape), vec, _const_spec(w_ple.shape),
                  _const_spec(w_gate.shape), vec],
        out_specs=tok(D_MODEL),
        out_shape=jax.ShapeDtypeStruct((n_tok, D_MODEL), F32),
        compiler_params=pltpu.CompilerParams(dimension_semantics=("parallel",),
                                             vmem_limit_bytes=VMEM_LIMIT),
        name="mlp_ple",
    )(x2d, p2d, g_pre, w_up, w_down, g_post, w_ple, w_gate, g_ple)


def _layer(x, p, layer_idx, prm):
    (norm_mix_pre, w_in, w_fourier, w_attn, w_out, lq1, lk1, lq2, lk2, subln,
     norm_mix_post, norm_mlp_pre, w_up, w_down, norm_mlp_post, w_ple, w_gate, norm_ple_post) = prm
    b, seq, _ = x.shape
    n_tok = b * seq
    row = lambda g: g.reshape(1, -1).astype(F32)
    x2d = x.reshape(n_tok, D_MODEL)

    fa, fb, q, k, v, gf, ga = _in_proj(x2d, row(norm_mix_pre), w_in, seq, tm=512)
    y2 = _seq_dft(fa, fb, b, seq, tcols=8192, group=8)

    lam_init = 0.8 - 0.6 * math.exp(-0.3 * layer_idx)
    lam = _lambda(lam_init, lq1, lk1, lq2, lk2)
    shp = (b, seq, Q_WIDTH)
    att = _diff_attn(lam, q.reshape(shp), k.reshape(shp), v.reshape(shp), row(subln),
                     1.0 - lam_init, tq=256, tk=1024, unroll=8, fin_group=4)

    x1 = _merge(x2d, y2, att.reshape(n_tok, ATTN_WIDTH), gf, ga, w_fourier, w_attn, w_out,
                row(norm_mix_post), tm=512)
    out = _mlp_ple(x1, p.reshape(n_tok, PLE_DIM), row(norm_mlp_pre), w_up, w_down,
                   row(norm_mlp_post), w_ple, w_gate, row(norm_ple_post), tm=256)
    return out.reshape(b, seq, D_MODEL)


def _prepare_weights(w_in, w_fourier, w_attn, w_out, w_up, w_down, w_ple, w_gate):
    w_in = w_in.astype(BF16)
    w_in = jnp.concatenate([_fold_chan_dft(w_in[:, :FOURIER_WIDTH]), w_in[:, FOURIER_WIDTH:]], axis=1)
    return (w_in,) + tuple(w.astype(BF16) for w in (w_fourier, w_attn, w_out, w_up, w_down, w_ple, w_gate))


def kernel(x_prompt, x_sample, p_prompt, p_sample, norm_mix_pre, w_in, w_fourier, w_attn, w_out,
           lambda_q1, lambda_k1, lambda_q2, lambda_k2, subln, norm_mix_post, norm_mlp_pre,
           w_up, w_down, norm_mlp_post, w_ple, w_ple_gate, norm_ple_post):
    y_prompt, y_sample = x_prompt, x_sample
    for i in range(w_in.shape[0]):
        wi, wf, wa, wo, wu, wd, wp, wg = _prepare_weights(
            w_in[i], w_fourier[i], w_attn[i], w_out[i], w_up[i], w_down[i], w_ple[i], w_ple_gate[i])
        prm = (norm_mix_pre[i], wi, wf, wa, wo,
               lambda_q1[i], lambda_k1[i], lambda_q2[i], lambda_k2[i], subln[i],
               norm_mix_post[i], norm_mlp_pre[i], wu, wd, norm_mlp_post[i], wp, wg, norm_ple_post[i])
        y_prompt = _layer(y_prompt, p_prompt[i], i, prm)
        y_sample = _layer(y_sample, p_sample[i], i, prm)
    return (y_prompt, y_sample)
```

```python
import functools
import math

import jax
import jax.numpy as jnp
import numpy as np
from jax import lax
from jax.experimental import pallas as pl
from jax.experimental.pallas import tpu as pltpu

D_MODEL = 1024
N_HEADS = 8
QK_HEAD_DIM = 64
V_HEAD_DIM = 128
Q_WIDTH = N_HEADS * 2 * QK_HEAD_DIM
ATTN_WIDTH = N_HEADS * V_HEAD_DIM
N_FOURIER_GROUPS = 4
FOURIER_GROUP_DIM = 128
FOURIER_WIDTH = N_FOURIER_GROUPS * FOURIER_GROUP_DIM
ROPE_DIM = QK_HEAD_DIM // 4
ROPE_THETA = 500000.0
D_FF = 4 * D_MODEL
PLE_DIM = 256
EPS = 1e-6

LANES = 128
VMEM_LIMIT = 56 * 1024 * 1024
F32 = jnp.float32
BF16 = jnp.bfloat16
LOG2E = 1.4426950408889634


def _rms(x, g):
    return x * lax.rsqrt(jnp.mean(x * x, axis=-1, keepdims=True) + EPS) * g


def _dot(a, b):
    return jnp.dot(a, b, preferred_element_type=F32)


def _sigmoid(x):
    return 1.0 / (1.0 + jnp.exp(-x))


def _const_spec(shape):
    nd = len(shape)
    return pl.BlockSpec(shape, lambda *_: (0,) * nd, pipeline_mode=pl.Buffered(1))


def _rope(y, c, sa, sb):
    outs = []
    for j in range(y.shape[1] // LANES):
        yc = y[:, j * LANES:(j + 1) * LANES]
        outs.append(yc * c + pltpu.roll(yc, ROPE_DIM // 2, 1) * sa
                    + pltpu.roll(yc, LANES - ROPE_DIM // 2, 1) * sb)
    return jnp.concatenate(outs, axis=1)


def _in_proj_kernel(x_ref, g_ref, w_ref, cq_ref, sqa_ref, sqb_ref, ck_ref, ska_ref, skb_ref,
                    fa_ref, fb_ref, q_ref, k_ref, v_ref, gf_ref, ga_ref):
    h = _rms(x_ref[...], g_ref[...]).astype(BF16)
    o = 0
    fa_ref[...] = _dot(h, w_ref[:, o:o + FOURIER_WIDTH]).astype(BF16); o += FOURIER_WIDTH
    fb_ref[...] = _dot(h, w_ref[:, o:o + FOURIER_WIDTH]).astype(BF16); o += FOURIER_WIDTH
    q = _dot(h, w_ref[:, o:o + Q_WIDTH]); o += Q_WIDTH
    q_ref[...] = _rope(q, cq_ref[...], sqa_ref[...], sqb_ref[...]).astype(BF16)
    k = _dot(h, w_ref[:, o:o + Q_WIDTH]); o += Q_WIDTH
    k_ref[...] = _rope(k, ck_ref[...], ska_ref[...], skb_ref[...]).astype(BF16)
    v_ref[...] = _dot(h, w_ref[:, o:o + ATTN_WIDTH]).astype(BF16); o += ATTN_WIDTH
    gf_ref[...] = _sigmoid(_dot(h, w_ref[:, o:o + D_MODEL])).astype(BF16); o += D_MODEL
    ga_ref[...] = _sigmoid(_dot(h, w_ref[:, o:o + D_MODEL])).astype(BF16)


def _rope_tables(seq, scale):
    half = ROPE_DIM // 2
    pos = jnp.arange(seq, dtype=F32)
    inv_freq = ROPE_THETA ** (-(jnp.arange(0, ROPE_DIM, 2, dtype=F32) / ROPE_DIM))
    ang = pos[:, None] * inv_freq[None, :]
    cos, sin = jnp.cos(ang), jnp.sin(ang)
    ones = jnp.ones((seq, QK_HEAD_DIM - ROPE_DIM), F32)
    zeros = jnp.zeros((seq, QK_HEAD_DIM - ROPE_DIM), F32)
    zh = jnp.zeros((seq, half), F32)
    c = jnp.concatenate([cos, cos, ones], axis=1)
    sa = jnp.concatenate([zh, sin, zeros], axis=1)
    sb = jnp.concatenate([-sin, zh, zeros], axis=1)
    rep = LANES // QK_HEAD_DIM
    return tuple(jnp.tile(t * scale, (1, rep)) for t in (c, sa, sb))


def _in_proj(x2d, gain, w_in, seq, tm):
    n_tok = x2d.shape[0]
    q_scale = LOG2E / math.sqrt(QK_HEAD_DIM)
    tabs = _rope_tables(seq, q_scale) + _rope_tables(seq, 1.0)
    spt = seq // tm
    tok = lambda w: pl.BlockSpec((tm, w), lambda i: (i, 0))
    tab = pl.BlockSpec((tm, LANES), lambda i: (i % spt, 0))
    widths = (FOURIER_WIDTH, FOURIER_WIDTH, Q_WIDTH, Q_WIDTH, ATTN_WIDTH, D_MODEL, D_MODEL)
    return pl.pallas_call(
        _in_proj_kernel,
        grid=(n_tok // tm,),
        in_specs=[tok(D_MODEL), _const_spec((1, D_MODEL)), _const_spec(w_in.shape)] + [tab] * 6,
        out_specs=[tok(w) for w in widths],
        out_shape=[jax.ShapeDtypeStruct((n_tok, w), BF16) for w in widths],
        compiler_params=pltpu.CompilerParams(dimension_semantics=("parallel",),
                                             vmem_limit_bytes=VMEM_LIMIT),
        name="in_proj",
    )(x2d, gain, w_in, *tabs)


def _chan_dft_weights():
    n = FOURIER_GROUP_DIM
    idx = np.arange(n)
    ang = 2.0 * np.pi * ((idx[:, None] * idx[None, :]) % n) / n
    eye = np.eye(N_FOURIER_GROUPS)
    c = np.kron(eye, np.cos(ang)) / math.sqrt(n)
    s = np.kron(eye, np.sin(ang)) / math.sqrt(n)
    return jnp.asarray(np.concatenate([c, s], axis=1), dtype=BF16)


def _fold_kernel(a_ref, b_ref, o_ref):
    o_ref[...] = _dot(a_ref[...], b_ref[...]).astype(BF16)


def _fold_chan_dft(w_f):
    return pl.pallas_call(
        _fold_kernel,
        out_shape=jax.ShapeDtypeStruct((w_f.shape[0], 2 * FOURIER_WIDTH), BF16),
        compiler_params=pltpu.CompilerParams(vmem_limit_bytes=VMEM_LIMIT),
        name="fold_chan_dft",
    )(w_f, _chan_dft_weights())


SEQ_DFT_N2 = 128


def _dft_stage1_kernel(a_ref, b_ref, m_ref, tc_ref, ts_ref, vr_ref, vi_ref):
    n1 = a_ref.shape[1]
    u = _dot(m_ref[...], jnp.concatenate([a_ref[0], b_ref[0]], axis=0))
    ur, ui = u[:n1], u[n1:]
    tc, ts = tc_ref[...], ts_ref[...]
    vr_ref[0] = (ur * tc - ui * ts).astype(BF16)
    vi_ref[0] = (ur * ts + ui * tc).astype(BF16)


def _dft_stage2_kernel(vr_ref, vi_ref, m_ref, y_ref):
    for i in range(vr_ref.shape[1]):
        v = jnp.concatenate([vr_ref[0, i], vi_ref[0, i]], axis=0)
        y_ref[0, :, i * FOURIER_WIDTH:(i + 1) * FOURIER_WIDTH] = _dot(m_ref[...], v).astype(BF16)


def _seq_dft(a, b, bsz, seq, tcols, group):
    n2 = SEQ_DFT_N2
    n1 = seq // n2
    cols = n2 * FOURIER_WIDTH
    j1 = np.arange(n1)
    ang1 = 2.0 * np.pi * ((j1[:, None] * j1[None, :]) % n1) / n1
    c1, s1 = np.cos(ang1), np.sin(ang1)
    m1 = jnp.asarray(np.block([[c1, -s1], [s1, c1]]) / math.sqrt(seq), dtype=BF16)
    j2 = np.arange(n2)
    ang2 = 2.0 * np.pi * ((j2[:, None] * j2[None, :]) % n2) / n2
    m3 = jnp.asarray(np.concatenate([np.cos(ang2), -np.sin(ang2)], axis=1), dtype=BF16)
    angt = (jnp.arange(n1, dtype=F32)[:, None] * jnp.arange(n2, dtype=F32)[None, :]) * (2.0 * math.pi / seq)
    tcos = jnp.repeat(jnp.cos(angt), FOURIER_WIDTH, axis=1)
    tsin = jnp.repeat(jnp.sin(angt), FOURIER_WIDTH, axis=1)

    blk = pl.BlockSpec((1, n1, tcols), lambda c, bi: (bi, 0, c))
    twd = pl.BlockSpec((n1, tcols), lambda c, bi: (0, c))
    vr, vi = pl.pallas_call(
        _dft_stage1_kernel,
        grid=(cols // tcols, bsz),
        in_specs=[blk, blk, _const_spec(m1.shape), twd, twd],
        out_specs=[blk, blk],
        out_shape=[jax.ShapeDtypeStruct((bsz, n1, cols), BF16)] * 2,
        compiler_params=pltpu.CompilerParams(dimension_semantics=("parallel", "parallel"),
                                             vmem_limit_bytes=VMEM_LIMIT),
        name="seq_dft_stage1",
    )(a.reshape(bsz, n1, cols), b.reshape(bsz, n1, cols), m1, tcos, tsin)

    vblk = pl.BlockSpec((1, group, n2, FOURIER_WIDTH), lambda bi, g: (bi, g, 0, 0))
    y = pl.pallas_call(
        _dft_stage2_kernel,
        grid=(bsz, n1 // group),
        in_specs=[vblk, vblk, _const_spec(m3.shape)],
        out_specs=pl.BlockSpec((1, n2, group * FOURIER_WIDTH), lambda bi, g: (bi, 0, g)),
        out_shape=jax.ShapeDtypeStruct((bsz, n2, n1 * FOURIER_WIDTH), BF16),
        compiler_params=pltpu.CompilerParams(dimension_semantics=("parallel", "parallel"),
                                             vmem_limit_bytes=VMEM_LIMIT),
        name="seq_dft_stage2",
    )(vr.reshape(bsz, n1, n2, FOURIER_WIDTH), vi.reshape(bsz, n1, n2, FOURIER_WIDTH), m3)
    return y.reshape(bsz * seq, FOURIER_WIDTH)


def _lambda_kernel(lam_init, q1_ref, k1_ref, q2_ref, k2_ref, o_ref):
    s1 = jnp.sum(q1_ref[...] * k1_ref[...], axis=-1, keepdims=True)
    s2 = jnp.sum(q2_ref[...] * k2_ref[...], axis=-1, keepdims=True)
    o_ref[...] = jnp.exp(s1) - jnp.exp(s2) + lam_init


def _lambda(lam_init, q1, k1, q2, k2):
    vec = lambda a: a.reshape(1, QK_HEAD_DIM).astype(F32)
    return pl.pallas_call(
        functools.partial(_lambda_kernel, lam_init),
        out_shape=jax.ShapeDtypeStruct((1, 1), F32),
        name="diff_lambda",
    )(vec(q1), vec(k1), vec(q2), vec(k2))


def _attn_kernel(tq, tk, unroll, fin_group, out_scale, lam_ref, q_ref, k_ref, v_ref, sub_ref, o_ref,
                 s0_ref, s1_ref, p0_ref, p1_ref, a0_ref, a1_ref, m0_ref, m1_ref, acc_ref):
    s_refs, p_refs, a_refs, m_refs = (s0_ref, s1_ref), (p0_ref, p1_ref), (a0_ref, a1_ref), (m0_ref, m1_ref)
    seq = q_ref.shape[1]
    nq, nk = seq // tq, seq // tk
    n_steps = nq * nk
    assert nk >= 2 and nk & (nk - 1) == 0 and nq % fin_group == 0
    assert unroll % 2 == 0 and n_steps % unroll == 0
    nk_shift = nk.bit_length() - 1
    lam = lam_ref[0, 0]
    lane = lax.broadcasted_iota(jnp.int32, (1, V_HEAD_DIM), 1)
    comp1 = lane < QK_HEAD_DIM
    nt = (((1,), (1,)), ((), ()))
    ones = jnp.ones((tk, V_HEAD_DIM), BF16)

    def split(t):
        return lax.shift_right_logical(t, nk_shift), lax.bitwise_and(t, nk - 1)

    def scores(t, slot):
        qi, kj = split(t)
        q = q_ref[0, pl.ds(pl.multiple_of(qi * tq, tq), tq), :]
        zero = jnp.zeros_like(q)
        qz = jnp.concatenate([jnp.where(comp1, q, zero), jnp.where(comp1, zero, q)], axis=0)
        kt = k_ref[0, pl.ds(pl.multiple_of(kj * tk, tk), tk), :]
        s_refs[slot][...] = lax.dot_general(qz, kt, nt, preferred_element_type=F32)

    def softmax(t, slot, can_be_first):
        _, kj = split(t)
        s_ref, m_in, m_out = s_refs[slot], m_refs[slot], m_refs[1 - slot]
        for g in range(2 * tq // 8):
            rows = slice(g * 8, (g + 1) * 8)
            blocks = [s_ref[rows, j * LANES:(j + 1) * LANES] for j in range(tk // LANES)]
            m_old = m_in[rows, :]
            if can_be_first:
                m_old = jnp.where(kj == 0, -jnp.inf, m_old)
            blk_max = functools.reduce(jnp.maximum, blocks)
            m_new = jnp.maximum(m_old, jnp.max(blk_max, axis=-1, keepdims=True))
            m_out[rows, :] = m_new
            a_refs[slot][rows, :] = jnp.exp2(m_old - m_new)
            p_refs[slot][rows, :] = jnp.concatenate(
                [jnp.exp2(b - m_new) for b in blocks], axis=1).astype(BF16)

    def pv(u, slot, tile, can_be_first):
        _, kj = split(u)
        vt = v_ref[0, pl.ds(pl.multiple_of(kj * tk, tk), tk), :]
        v1 = jnp.concatenate([vt, ones], axis=1)
        alpha = a_refs[slot][...]
        alpha2 = jnp.concatenate([alpha, alpha], axis=1)
        acc = acc_ref[tile]
        if can_be_first:
            acc = jnp.where(kj == 0, 0.0, acc)
        acc_ref[tile] = alpha2 * acc + _dot(p_refs[slot][...], v1)

    @pl.when((pl.program_id(0) == 0) & (pl.program_id(1) == 0))
    def _():
        acc_ref[...] = jnp.zeros(acc_ref.shape, F32)
        m0_ref[...] = jnp.zeros(m0_ref.shape, F32)

    scores(0, 0)

    def body(i, _):
        t = unroll * i
        for j in range(unroll):
            scores(jnp.minimum(t + j + 1, n_steps - 1), (j + 1) % 2)
            first = j % math.gcd(unroll, nk) == 0
            softmax(t + j, j % 2, first)
            pv(t + j, j % 2, split(t + j)[0], first)
        return 0

    lax.fori_loop(0, n_steps // unroll, body, 0)

    sub = sub_ref[...] * out_scale

    def finish(i, _):
        for j in range(fin_group):
            tile = i * fin_group + j
            a = acc_ref[tile]
            o = a[:, :V_HEAD_DIM] / a[:, V_HEAD_DIM:]
            att = o[:tq] - lam * o[tq:]
            o_ref[0, pl.ds(pl.multiple_of(tile * tq, tq), tq), :] = _rms(att, sub).astype(BF16)
        return 0

    lax.fori_loop(0, nq // fin_group, finish, 0)


def _diff_attn(lam, q, k, v, subln, out_scale, tq, tk, unroll, fin_group):
    b, seq, _ = q.shape
    head = pl.BlockSpec((1, seq, V_HEAD_DIM), lambda bi, h: (bi, 0, h))
    return pl.pallas_call(
        functools.partial(_attn_kernel, tq, tk, unroll, fin_group, out_scale),
        grid=(b, N_HEADS),
        in_specs=[pl.BlockSpec(memory_space=pltpu.SMEM), head, head, head,
                  _const_spec((1, V_HEAD_DIM))],
        out_specs=head,
        out_shape=jax.ShapeDtypeStruct((b, seq, ATTN_WIDTH), BF16),
        scratch_shapes=[pltpu.VMEM((2 * tq, tk), F32)] * 2 + [pltpu.VMEM((2 * tq, tk), BF16)] * 2
        + [pltpu.VMEM((2 * tq, LANES), F32)] * 4
        + [pltpu.VMEM((seq // tq, 2 * tq, 2 * V_HEAD_DIM), F32)],
        compiler_params=pltpu.CompilerParams(dimension_semantics=("arbitrary", "arbitrary"),
                                             vmem_limit_bytes=VMEM_LIMIT),
        name="diff_attn",
    )(lam, q, k, v, subln)


def _merge_kernel(x_ref, y_ref, att_ref, gf_ref, ga_ref, wf_ref, wa_ref, wo_ref, g_ref, o_ref):
    fo = _dot(y_ref[...], wf_ref[...])
    ao = _dot(att_ref[...], wa_ref[...])
    merged = gf_ref[...].astype(F32) * fo + ga_ref[...].astype(F32) * ao
    o_ref[...] = x_ref[...] + _rms(_dot(merged.astype(BF16), wo_ref[...]), g_ref[...])


def _merge(x2d, y2, att2d, gf, ga, w_fourier, w_attn, w_out, gain, tm):
    n_tok = x2d.shape[0]
    tok = lambda w: pl.BlockSpec((tm, w), lambda i: (i, 0))
    return pl.pallas_call(
        _merge_kernel,
        grid=(n_tok // tm,),
        in_specs=[tok(D_MODEL), tok(FOURIER_WIDTH), tok(ATTN_WIDTH), tok(D_MODEL), tok(D_MODEL),
                  _const_spec(w_fourier.shape), _const_spec(w_attn.shape),
                  _const_spec(w_out.shape), _const_spec((1, D_MODEL))],
        out_specs=tok(D_MODEL),
        out_shape=jax.ShapeDtypeStruct((n_tok, D_MODEL), F32),
        compiler_params=pltpu.CompilerParams(dimension_semantics=("parallel",),
                                             vmem_limit_bytes=VMEM_LIMIT),
        name="merge",
    )(x2d, y2, att2d, gf, ga, w_fourier, w_attn, w_out, gain)


def _mlp_ple_kernel(x_ref, p_ref, gpre_ref, wup_ref, wdn_ref, gpost_ref, wple_ref, wgate_ref,
                    gple_ref, o_ref):
    x = x_ref[...]
    h = _rms(x, gpre_ref[...]).astype(BF16)
    d = jnp.zeros(x.shape, F32)
    for c in range(D_FF // D_MODEL):
        cols = slice(c * D_MODEL, (c + 1) * D_MODEL)
        u = jnp.square(jnp.maximum(_dot(h, wup_ref[:, cols]), 0.0))
        d = d + _dot(u.astype(BF16), wdn_ref[cols, :])
    x = x + _rms(d, gpost_ref[...])
    e = _dot(p_ref[...].astype(BF16), wple_ref[...]) * _sigmoid(_dot(x.astype(BF16), wgate_ref[...]))
    o_ref[...] = x + _rms(e, gple_ref[...])


def _mlp_ple(x2d, p2d, g_pre, w_up, w_down, g_post, w_ple, w_gate, g_ple, tm):
    n_tok = x2d.shape[0]
    tok = lambda w: pl.BlockSpec((tm, w), lambda i: (i, 0))
    vec = _const_spec((1, D_MODEL))
    return pl.pallas_call(
        _mlp_ple_kernel,
        grid=(n_tok // tm,),
        in_specs=[tok(D_MODEL), tok(PLE_DIM), vec, _const_spec(w_up.shape),
                  _const_spec(w_down.shape), vec, _const_spec(w_ple.shape),
                  _const_spec(w_gate.shape), vec],
        out_specs=tok(D_MODEL),
        out_shape=jax.ShapeDtypeStruct((n_tok, D_MODEL), F32),
        compiler_params=pltpu.CompilerParams(dimension_semantics=("parallel",),
                                             vmem_limit_bytes=VMEM_LIMIT),
        name="mlp_ple",
    )(x2d, p2d, g_pre, w_up, w_down, g_post, w_ple, w_gate, g_ple)


def _layer(x, p, layer_idx, prm):
    (norm_mix_pre, w_in, w_fourier, w_attn, w_out, lq1, lk1, lq2, lk2, subln,
     norm_mix_post, norm_mlp_pre, w_up, w_down, norm_mlp_post, w_ple, w_gate, norm_ple_post) = prm
    b, seq, _ = x.shape
    n_tok = b * seq
    row = lambda g: g.reshape(1, -1).astype(F32)
    x2d = x.reshape(n_tok, D_MODEL)

    fa, fb, q, k, v, gf, ga = _in_proj(x2d, row(norm_mix_pre), w_in, seq, tm=512)
    y2 = _seq_dft(fa, fb, b, seq, tcols=(1 << 19) * SEQ_DFT_N2 // seq, group=8)

    lam_init = 0.8 - 0.6 * math.exp(-0.3 * layer_idx)
    lam = _lambda(lam_init, lq1, lk1, lq2, lk2)
    shp = (b, seq, Q_WIDTH)
    att = _diff_attn(lam, q.reshape(shp), k.reshape(shp), v.reshape(shp), row(subln),
                     1.0 - lam_init, tq=256, tk=1024, unroll=8, fin_group=4)

    x1 = _merge(x2d, y2, att.reshape(n_tok, ATTN_WIDTH), gf, ga, w_fourier, w_attn, w_out,
                row(norm_mix_post), tm=512)
    out = _mlp_ple(x1, p.reshape(n_tok, PLE_DIM), row(norm_mlp_pre), w_up, w_down,
                   row(norm_mlp_post), w_ple, w_gate, row(norm_ple_post), tm=512)
    return out.reshape(b, seq, D_MODEL)


def _prepare_weights(w_in, w_fourier, w_attn, w_out, w_up, w_down, w_ple, w_gate):
    w_in = w_in.astype(BF16)
    w_in = jnp.concatenate([_fold_chan_dft(w_in[:, :FOURIER_WIDTH]), w_in[:, FOURIER_WIDTH:]], axis=1)
    return (w_in,) + tuple(w.astype(BF16) for w in (w_fourier, w_attn, w_out, w_up, w_down, w_ple, w_gate))


def kernel(x_prompt, x_sample, p_prompt, p_sample, norm_mix_pre, w_in, w_fourier, w_attn, w_out,
           lambda_q1, lambda_k1, lambda_q2, lambda_k2, subln, norm_mix_post, norm_mlp_pre,
           w_up, w_down, norm_mlp_post, w_ple, w_ple_gate, norm_ple_post):
    y_prompt, y_sample = x_prompt, x_sample
    for i in range(w_in.shape[0]):
        wi, wf, wa, wo, wu, wd, wp, wg = _prepare_weights(
            w_in[i], w_fourier[i], w_attn[i], w_out[i], w_up[i], w_down[i], w_ple[i], w_ple_gate[i])
        prm = (norm_mix_pre[i], wi, wf, wa, wo,
               lambda_q1[i], lambda_k1[i], lambda_q2[i], lambda_k2[i], subln[i],
               norm_mix_post[i], norm_mlp_pre[i], wu, wd, norm_mlp_post[i], wp, wg, norm_ple_post[i])
        y_prompt = _layer(y_prompt, p_prompt[i], i, prm)
        y_sample = _layer(y_sample, p_sample[i], i, prm)
    return (y_prompt, y_sample)
```

```python
import functools
import math

import jax
import jax.numpy as jnp
import numpy as np
from jax import lax
from jax.experimental import pallas as pl
from jax.experimental.pallas import tpu as pltpu

D_MODEL = 1024
N_HEADS = 8
QK_HEAD_DIM = 64
V_HEAD_DIM = 128
Q_WIDTH = N_HEADS * 2 * QK_HEAD_DIM
ATTN_WIDTH = N_HEADS * V_HEAD_DIM
N_FOURIER_GROUPS = 4
FOURIER_GROUP_DIM = 128
FOURIER_WIDTH = N_FOURIER_GROUPS * FOURIER_GROUP_DIM
ROPE_DIM = QK_HEAD_DIM // 4
ROPE_THETA = 500000.0
D_FF = 4 * D_MODEL
PLE_DIM = 256
EPS = 1e-6

LANES = 128
VMEM_LIMIT = 56 * 1024 * 1024
F32 = jnp.float32
BF16 = jnp.bfloat16
LOG2E = 1.4426950408889634


def _rms(x, g):
    return x * lax.rsqrt(jnp.mean(x * x, axis=-1, keepdims=True) + EPS) * g


def _dot(a, b):
    return jnp.dot(a, b, preferred_element_type=F32)


def _sigmoid(x):
    return 1.0 / (1.0 + jnp.exp(-x))


def _const_spec(shape):
    nd = len(shape)
    return pl.BlockSpec(shape, lambda *_: (0,) * nd, pipeline_mode=pl.Buffered(1))


def _rope(y, c, sa, sb):
    outs = []
    for j in range(y.shape[1] // LANES):
        yc = y[:, j * LANES:(j + 1) * LANES]
        outs.append(yc * c + pltpu.roll(yc, ROPE_DIM // 2, 1) * sa
                    + pltpu.roll(yc, LANES - ROPE_DIM // 2, 1) * sb)
    return jnp.concatenate(outs, axis=1)


def _in_proj_kernel(x_ref, g_ref, w_ref, cq_ref, sqa_ref, sqb_ref, ck_ref, ska_ref, skb_ref,
                    fa_ref, fb_ref, q_ref, k_ref, v_ref, gf_ref, ga_ref):
    h = _rms(x_ref[...], g_ref[...]).astype(BF16)
    o = 0
    fa_ref[...] = _dot(h, w_ref[:, o:o + FOURIER_WIDTH]).astype(BF16); o += FOURIER_WIDTH
    fb_ref[...] = _dot(h, w_ref[:, o:o + FOURIER_WIDTH]).astype(BF16); o += FOURIER_WIDTH
    q = _dot(h, w_ref[:, o:o + Q_WIDTH]); o += Q_WIDTH
    q_ref[...] = _rope(q, cq_ref[...], sqa_ref[...], sqb_ref[...]).astype(BF16)
    k = _dot(h, w_ref[:, o:o + Q_WIDTH]); o += Q_WIDTH
    k_ref[...] = _rope(k, ck_ref[...], ska_ref[...], skb_ref[...]).astype(BF16)
    v_ref[...] = _dot(h, w_ref[:, o:o + ATTN_WIDTH]).astype(BF16); o += ATTN_WIDTH
    gf_ref[...] = _sigmoid(_dot(h, w_ref[:, o:o + D_MODEL])).astype(BF16); o += D_MODEL
    ga_ref[...] = _sigmoid(_dot(h, w_ref[:, o:o + D_MODEL])).astype(BF16)


def _rope_tables(seq, scale):
    half = ROPE_DIM // 2
    pos = jnp.arange(seq, dtype=F32)
    inv_freq = ROPE_THETA ** (-(jnp.arange(0, ROPE_DIM, 2, dtype=F32) / ROPE_DIM))
    ang = pos[:, None] * inv_freq[None, :]
    cos, sin = jnp.cos(ang), jnp.sin(ang)
    ones = jnp.ones((seq, QK_HEAD_DIM - ROPE_DIM), F32)
    zeros = jnp.zeros((seq, QK_HEAD_DIM - ROPE_DIM), F32)
    zh = jnp.zeros((seq, half), F32)
    c = jnp.concatenate([cos, cos, ones], axis=1)
    sa = jnp.concatenate([zh, sin, zeros], axis=1)
    sb = jnp.concatenate([-sin, zh, zeros], axis=1)
    rep = LANES // QK_HEAD_DIM
    return tuple(jnp.tile(t * scale, (1, rep)) for t in (c, sa, sb))


def _in_proj(x2d, gain, w_in, seq, tm):
    n_tok = x2d.shape[0]
    q_scale = LOG2E / math.sqrt(QK_HEAD_DIM)
    tabs = _rope_tables(seq, q_scale) + _rope_tables(seq, 1.0)
    spt = seq // tm
    tok = lambda w: pl.BlockSpec((tm, w), lambda i: (i, 0))
    tab = pl.BlockSpec((tm, LANES), lambda i: (i % spt, 0))
    widths = (FOURIER_WIDTH, FOURIER_WIDTH, Q_WIDTH, Q_WIDTH, ATTN_WIDTH, D_MODEL, D_MODEL)
    return pl.pallas_call(
        _in_proj_kernel,
        grid=(n_tok // tm,),
        in_specs=[tok(D_MODEL), _const_spec((1, D_MODEL)), _const_spec(w_in.shape)] + [tab] * 6,
        out_specs=[tok(w) for w in widths],
        out_shape=[jax.ShapeDtypeStruct((n_tok, w), BF16) for w in widths],
        compiler_params=pltpu.CompilerParams(dimension_semantics=("parallel",),
                                             vmem_limit_bytes=VMEM_LIMIT),
        name="in_proj",
    )(x2d, gain, w_in, *tabs)


def _chan_dft_weights():
    n = FOURIER_GROUP_DIM
    idx = np.arange(n)
    ang = 2.0 * np.pi * ((idx[:, None] * idx[None, :]) % n) / n
    eye = np.eye(N_FOURIER_GROUPS)
    c = np.kron(eye, np.cos(ang)) / math.sqrt(n)
    s = np.kron(eye, np.sin(ang)) / math.sqrt(n)
    return jnp.asarray(np.concatenate([c, s], axis=1), dtype=BF16)


def _fold_kernel(a_ref, b_ref, o_ref):
    o_ref[...] = _dot(a_ref[...], b_ref[...]).astype(BF16)


def _fold_chan_dft(w_f):
    return pl.pallas_call(
        _fold_kernel,
        out_shape=jax.ShapeDtypeStruct((w_f.shape[0], 2 * FOURIER_WIDTH), BF16),
        compiler_params=pltpu.CompilerParams(vmem_limit_bytes=VMEM_LIMIT),
        name="fold_chan_dft",
    )(w_f, _chan_dft_weights())


def _dense_dft_kernel(a_ref, b_ref, c_ref, s_ref, y_ref):
    y_ref[0] = (_dot(c_ref[...], a_ref[0]) + _dot(s_ref[...], b_ref[0])).astype(BF16)


def _seq_dft_dense(a, b, bsz, seq):
    j = jnp.arange(seq, dtype=jnp.int32)
    ang = ((j[:, None] * j[None, :]) % seq).astype(F32) * (2.0 * math.pi / seq)
    sc = 1.0 / math.sqrt(seq)
    c, s = (jnp.cos(ang) * sc).astype(BF16), (jnp.sin(ang) * (-sc)).astype(BF16)
    blk = pl.BlockSpec((1, seq, FOURIER_WIDTH), lambda bi: (bi, 0, 0))
    shp = (bsz, seq, FOURIER_WIDTH)
    y = pl.pallas_call(
        _dense_dft_kernel,
        grid=(bsz,),
        in_specs=[blk, blk, _const_spec(c.shape), _const_spec(s.shape)],
        out_specs=blk,
        out_shape=jax.ShapeDtypeStruct(shp, BF16),
        compiler_params=pltpu.CompilerParams(dimension_semantics=("parallel",),
                                             vmem_limit_bytes=VMEM_LIMIT),
        name="seq_dft_dense",
    )(a.reshape(shp), b.reshape(shp), c, s)
    return y.reshape(bsz * seq, FOURIER_WIDTH)


SEQ_DFT_N2 = 128
DENSE_DFT_MAX_SEQ = 2048


def _dft_stage1_kernel(a_ref, b_ref, m_ref, tc_ref, ts_ref, vr_ref, vi_ref):
    n1 = a_ref.shape[1]
    u = _dot(m_ref[...], jnp.concatenate([a_ref[0], b_ref[0]], axis=0))
    ur, ui = u[:n1], u[n1:]
    tc, ts = tc_ref[...], ts_ref[...]
    vr_ref[0] = (ur * tc - ui * ts).astype(BF16)
    vi_ref[0] = (ur * ts + ui * tc).astype(BF16)


def _dft_stage2_kernel(vr_ref, vi_ref, m_ref, y_ref):
    for i in range(vr_ref.shape[1]):
        v = jnp.concatenate([vr_ref[0, i], vi_ref[0, i]], axis=0)
        y_ref[0, :, i * FOURIER_WIDTH:(i + 1) * FOURIER_WIDTH] = _dot(m_ref[...], v).astype(BF16)


def _seq_dft(a, b, bsz, seq, tcols, group):
    n2 = SEQ_DFT_N2
    n1 = seq // n2
    cols = n2 * FOURIER_WIDTH
    j1 = np.arange(n1)
    ang1 = 2.0 * np.pi * ((j1[:, None] * j1[None, :]) % n1) / n1
    c1, s1 = np.cos(ang1), np.sin(ang1)
    m1 = jnp.asarray(np.block([[c1, -s1], [s1, c1]]) / math.sqrt(seq), dtype=BF16)
    j2 = np.arange(n2)
    ang2 = 2.0 * np.pi * ((j2[:, None] * j2[None, :]) % n2) / n2
    m3 = jnp.asarray(np.concatenate([np.cos(ang2), -np.sin(ang2)], axis=1), dtype=BF16)
    angt = (jnp.arange(n1, dtype=F32)[:, None] * jnp.arange(n2, dtype=F32)[None, :]) * (2.0 * math.pi / seq)
    tcos = jnp.repeat(jnp.cos(angt), FOURIER_WIDTH, axis=1)
    tsin = jnp.repeat(jnp.sin(angt), FOURIER_WIDTH, axis=1)

    blk = pl.BlockSpec((1, n1, tcols), lambda c, bi: (bi, 0, c))
    twd = pl.BlockSpec((n1, tcols), lambda c, bi: (0, c))
    vr, vi = pl.pallas_call(
        _dft_stage1_kernel,
        grid=(cols // tcols, bsz),
        in_specs=[blk, blk, _const_spec(m1.shape), twd, twd],
        out_specs=[blk, blk],
        out_shape=[jax.ShapeDtypeStruct((bsz, n1, cols), BF16)] * 2,
        compiler_params=pltpu.CompilerParams(dimension_semantics=("parallel", "parallel"),
                                             vmem_limit_bytes=VMEM_LIMIT),
        name="seq_dft_stage1",
    )(a.reshape(bsz, n1, cols), b.reshape(bsz, n1, cols), m1, tcos, tsin)

    vblk = pl.BlockSpec((1, group, n2, FOURIER_WIDTH), lambda bi, g: (bi, g, 0, 0))
    y = pl.pallas_call(
        _dft_stage2_kernel,
        grid=(bsz, n1 // group),
        in_specs=[vblk, vblk, _const_spec(m3.shape)],
        out_specs=pl.BlockSpec((1, n2, group * FOURIER_WIDTH), lambda bi, g: (bi, 0, g)),
        out_shape=jax.ShapeDtypeStruct((bsz, n2, n1 * FOURIER_WIDTH), BF16),
        compiler_params=pltpu.CompilerParams(dimension_semantics=("parallel", "parallel"),
                                             vmem_limit_bytes=VMEM_LIMIT),
        name="seq_dft_stage2",
    )(vr.reshape(bsz, n1, n2, FOURIER_WIDTH), vi.reshape(bsz, n1, n2, FOURIER_WIDTH), m3)
    return y.reshape(bsz * seq, FOURIER_WIDTH)


def _lambda_kernel(lam_init, q1_ref, k1_ref, q2_ref, k2_ref, o_ref):
    s1 = jnp.sum(q1_ref[...] * k1_ref[...], axis=-1, keepdims=True)
    s2 = jnp.sum(q2_ref[...] * k2_ref[...], axis=-1, keepdims=True)
    o_ref[...] = jnp.exp(s1) - jnp.exp(s2) + lam_init


def _lambda(lam_init, q1, k1, q2, k2):
    vec = lambda a: a.reshape(1, QK_HEAD_DIM).astype(F32)
    return pl.pallas_call(
        functools.partial(_lambda_kernel, lam_init),
        out_shape=jax.ShapeDtypeStruct((1, 1), F32),
        name="diff_lambda",
    )(vec(q1), vec(k1), vec(q2), vec(k2))


def _attn_kernel(tq, tk, unroll, fin_group, out_scale, lam_ref, q_ref, k_ref, v_ref, sub_ref, o_ref,
                 s0_ref, s1_ref, p0_ref, p1_ref, a0_ref, a1_ref, m0_ref, m1_ref, acc_ref):
    s_refs, p_refs, a_refs, m_refs = (s0_ref, s1_ref), (p0_ref, p1_ref), (a0_ref, a1_ref), (m0_ref, m1_ref)
    seq = q_ref.shape[1]
    nq, nk = seq // tq, seq // tk
    n_steps = nq * nk
    assert nk & (nk - 1) == 0 and nq % fin_group == 0
    assert unroll % 2 == 0 and n_steps % unroll == 0
    nk_shift = nk.bit_length() - 1
    lam = lam_ref[0, 0]
    lane = lax.broadcasted_iota(jnp.int32, (1, V_HEAD_DIM), 1)
    comp1 = lane < QK_HEAD_DIM
    nt = (((1,), (1,)), ((), ()))
    ones = jnp.ones((tk, V_HEAD_DIM), BF16)

    def split(t):
        return lax.shift_right_logical(t, nk_shift), lax.bitwise_and(t, nk - 1)

    def scores(t, slot):
        qi, kj = split(t)
        q = q_ref[0, pl.ds(pl.multiple_of(qi * tq, tq), tq), :]
        zero = jnp.zeros_like(q)
        qz = jnp.concatenate([jnp.where(comp1, q, zero), jnp.where(comp1, zero, q)], axis=0)
        kt = k_ref[0, pl.ds(pl.multiple_of(kj * tk, tk), tk), :]
        s_refs[slot][...] = lax.dot_general(qz, kt, nt, preferred_element_type=F32)

    def softmax(t, slot, can_be_first):
        _, kj = split(t)
        s_ref, m_in, m_out = s_refs[slot], m_refs[slot], m_refs[1 - slot]
        for g in range(2 * tq // 8):
            rows = slice(g * 8, (g + 1) * 8)
            blocks = [s_ref[rows, j * LANES:(j + 1) * LANES] for j in range(tk // LANES)]
            m_old = m_in[rows, :]
            if can_be_first:
                m_old = jnp.where(kj == 0, -jnp.inf, m_old)
            blk_max = functools.reduce(jnp.maximum, blocks)
            m_new = jnp.maximum(m_old, jnp.max(blk_max, axis=-1, keepdims=True))
            m_out[rows, :] = m_new
            a_refs[slot][rows, :] = jnp.exp2(m_old - m_new)
            p_refs[slot][rows, :] = jnp.concatenate(
                [jnp.exp2(b - m_new) for b in blocks], axis=1).astype(BF16)

    def pv(u, slot, tile, can_be_first):
        _, kj = split(u)
        vt = v_ref[0, pl.ds(pl.multiple_of(kj * tk, tk), tk), :]
        v1 = jnp.concatenate([vt, ones], axis=1)
        alpha = a_refs[slot][...]
        alpha2 = jnp.concatenate([alpha, alpha], axis=1)
        acc = acc_ref[tile]
        if can_be_first:
            acc = jnp.where(kj == 0, 0.0, acc)
        acc_ref[tile] = alpha2 * acc + _dot(p_refs[slot][...], v1)

    @pl.when((pl.program_id(0) == 0) & (pl.program_id(1) == 0))
    def _():
        acc_ref[...] = jnp.zeros(acc_ref.shape, F32)
        m0_ref[...] = jnp.zeros(m0_ref.shape, F32)

    scores(0, 0)

    def body(i, _):
        t = unroll * i
        for j in range(unroll):
            scores(jnp.minimum(t + j + 1, n_steps - 1), (j + 1) % 2)
            first = j % math.gcd(unroll, nk) == 0
            softmax(t + j, j % 2, first)
            pv(t + j, j % 2, split(t + j)[0], first)
        return 0

    lax.fori_loop(0, n_steps // unroll, body, 0)

    sub = sub_ref[...] * out_scale

    def finish(i, _):
        for j in range(fin_group):
            tile = i * fin_group + j
            a = acc_ref[tile]
            o = a[:, :V_HEAD_DIM] / a[:, V_HEAD_DIM:]
            att = o[:tq] - lam * o[tq:]
            o_ref[0, pl.ds(pl.multiple_of(tile * tq, tq), tq), :] = _rms(att, sub).astype(BF16)
        return 0

    lax.fori_loop(0, nq // fin_group, finish, 0)


def _diff_attn(lam, q, k, v, subln, out_scale, tq, tk, unroll, fin_group):
    b, seq, _ = q.shape
    head = pl.BlockSpec((1, seq, V_HEAD_DIM), lambda bi, h: (bi, 0, h))
    return pl.pallas_call(
        functools.partial(_attn_kernel, tq, tk, unroll, fin_group, out_scale),
        grid=(b, N_HEADS),
        in_specs=[pl.BlockSpec(memory_space=pltpu.SMEM), head, head, head,
                  _const_spec((1, V_HEAD_DIM))],
        out_specs=head,
        out_shape=jax.ShapeDtypeStruct((b, seq, ATTN_WIDTH), BF16),
        scratch_shapes=[pltpu.VMEM((2 * tq, tk), F32)] * 2 + [pltpu.VMEM((2 * tq, tk), BF16)] * 2
        + [pltpu.VMEM((2 * tq, LANES), F32)] * 4
        + [pltpu.VMEM((seq // tq, 2 * tq, 2 * V_HEAD_DIM), F32)],
        compiler_params=pltpu.CompilerParams(dimension_semantics=("arbitrary", "arbitrary"),
                                             vmem_limit_bytes=VMEM_LIMIT),
        name="diff_attn",
    )(lam, q, k, v, subln)


def _merge_kernel(x_ref, y_ref, att_ref, gf_ref, ga_ref, wf_ref, wa_ref, wo_ref, g_ref, o_ref):
    fo = _dot(y_ref[...], wf_ref[...])
    ao = _dot(att_ref[...], wa_ref[...])
    merged = gf_ref[...].astype(F32) * fo + ga_ref[...].astype(F32) * ao
    o_ref[...] = x_ref[...] + _rms(_dot(merged.astype(BF16), wo_ref[...]), g_ref[...])


def _merge(x2d, y2, att2d, gf, ga, w_fourier, w_attn, w_out, gain, tm):
    n_tok = x2d.shape[0]
    tok = lambda w: pl.BlockSpec((tm, w), lambda i: (i, 0))
    return pl.pallas_call(
        _merge_kernel,
        grid=(n_tok // tm,),
        in_specs=[tok(D_MODEL), tok(FOURIER_WIDTH), tok(ATTN_WIDTH), tok(D_MODEL), tok(D_MODEL),
                  _const_spec(w_fourier.shape), _const_spec(w_attn.shape),
                  _const_spec(w_out.shape), _const_spec((1, D_MODEL))],
        out_specs=tok(D_MODEL),
        out_shape=jax.ShapeDtypeStruct((n_tok, D_MODEL), F32),
        compiler_params=pltpu.CompilerParams(dimension_semantics=("parallel",),
                                             vmem_limit_bytes=VMEM_LIMIT),
        name="merge",
    )(x2d, y2, att2d, gf, ga, w_fourier, w_attn, w_out, gain)


def _mlp_ple_kernel(x_ref, p_ref, gpre_ref, wup_ref, wdn_ref, gpost_ref, wple_ref, wgate_ref,
                    gple_ref, o_ref):
    x = x_ref[...]
    h = _rms(x, gpre_ref[...]).astype(BF16)
    d = jnp.zeros(x.shape, F32)
    for c in range(D_FF // D_MODEL):
        cols = slice(c * D_MODEL, (c + 1) * D_MODEL)
        u = jnp.square(jnp.maximum(_dot(h, wup_ref[:, cols]), 0.0))
        d = d + _dot(u.astype(BF16), wdn_ref[cols, :])
    x = x + _rms(d, gpost_ref[...])
    e = _dot(p_ref[...].astype(BF16), wple_ref[...]) * _sigmoid(_dot(x.astype(BF16), wgate_ref[...]))
    o_ref[...] = x + _rms(e, gple_ref[...])


def _mlp_ple(x2d, p2d, g_pre, w_up, w_down, g_post, w_ple, w_gate, g_ple, tm):
    n_tok = x2d.shape[0]
    tok = lambda w: pl.BlockSpec((tm, w), lambda i: (i, 0))
    vec = _const_spec((1, D_MODEL))
    return pl.pallas_call(
        _mlp_ple_kernel,
        grid=(n_tok // tm,),
        in_specs=[tok(D_MODEL), tok(PLE_DIM), vec, _const_spec(w_up.shape),
                  _const_spec(w_down.shape), vec, _const_spec(w_ple.shape),
                  _const_spec(w_gate.shape), vec],
        out_specs=tok(D_MODEL),
        out_shape=jax.ShapeDtypeStruct((n_tok, D_MODEL), F32),
        compiler_params=pltpu.CompilerParams(dimension_semantics=("parallel",),
                                             vmem_limit_bytes=VMEM_LIMIT),
        name="mlp_ple",
    )(x2d, p2d, g_pre, w_up, w_down, g_post, w_ple, w_gate, g_ple)


def _layer(x, p, layer_idx, prm):
    (norm_mix_pre, w_in, w_fourier, w_attn, w_out, lq1, lk1, lq2, lk2, subln,
     norm_mix_post, norm_mlp_pre, w_up, w_down, norm_mlp_post, w_ple, w_gate, norm_ple_post) = prm
    b, seq, _ = x.shape
    n_tok = b * seq
    row = lambda g: g.reshape(1, -1).astype(F32)
    x2d = x.reshape(n_tok, D_MODEL)

    fa, fb, q, k, v, gf, ga = _in_proj(x2d, row(norm_mix_pre), w_in, seq, tm=512)
    if seq <= DENSE_DFT_MAX_SEQ:
        y2 = _seq_dft_dense(fa, fb, b, seq)
    else:
        y2 = _seq_dft(fa, fb, b, seq, tcols=(1 << 19) * SEQ_DFT_N2 // seq, group=8)

    lam_init = 0.8 - 0.6 * math.exp(-0.3 * layer_idx)
    lam = _lambda(lam_init, lq1, lk1, lq2, lk2)
    shp = (b, seq, Q_WIDTH)
    att = _diff_attn(lam, q.reshape(shp), k.reshape(shp), v.reshape(shp), row(subln),
                     1.0 - lam_init, tq=256, tk=1024, unroll=8, fin_group=4)

    x1 = _merge(x2d, y2, att.reshape(n_tok, ATTN_WIDTH), gf, ga, w_fourier, w_attn, w_out,
                row(norm_mix_post), tm=512)
    out = _mlp_ple(x1, p.reshape(n_tok, PLE_DIM), row(norm_mlp_pre), w_up, w_down,
                   row(norm_mlp_post), w_ple, w_gate, row(norm_ple_post), tm=512)
    return out.reshape(b, seq, D_MODEL)


def _prepare_weights(w_in, w_fourier, w_attn, w_out, w_up, w_down, w_ple, w_gate):
    w_in = w_in.astype(BF16)
    w_in = jnp.concatenate([_fold_chan_dft(w_in[:, :FOURIER_WIDTH]), w_in[:, FOURIER_WIDTH:]], axis=1)
    return (w_in,) + tuple(w.astype(BF16) for w in (w_fourier, w_attn, w_out, w_up, w_down, w_ple, w_gate))


def kernel(x_prompt, x_sample, p_prompt, p_sample, norm_mix_pre, w_in, w_fourier, w_attn, w_out,
           lambda_q1, lambda_k1, lambda_q2, lambda_k2, subln, norm_mix_post, norm_mlp_pre,
           w_up, w_down, norm_mlp_post, w_ple, w_ple_gate, norm_ple_post):
    y_prompt, y_sample = x_prompt, x_sample
    for i in range(w_in.shape[0]):
        wi, wf, wa, wo, wu, wd, wp, wg = _prepare_weights(
            w_in[i], w_fourier[i], w_attn[i], w_out[i], w_up[i], w_down[i], w_ple[i], w_ple_gate[i])
        prm = (norm_mix_pre[i], wi, wf, wa, wo,
               lambda_q1[i], lambda_k1[i], lambda_q2[i], lambda_k2[i], subln[i],
               norm_mix_post[i], norm_mlp_pre[i], wu, wd, norm_mlp_post[i], wp, wg, norm_ple_post[i])
        y_prompt = _layer(y_prompt, p_prompt[i], i, prm)
        y_sample = _layer(y_sample, p_sample[i], i, prm)
    return (y_prompt, y_sample)
```

```python
import functools
import math

import jax
import jax.numpy as jnp
import numpy as np
from jax import lax
from jax.experimental import pallas as pl
from jax.experimental.pallas import tpu as pltpu

D_MODEL = 1024
N_HEADS = 8
QK_HEAD_DIM = 64
V_HEAD_DIM = 128
Q_WIDTH = N_HEADS * 2 * QK_HEAD_DIM
ATTN_WIDTH = N_HEADS * V_HEAD_DIM
N_FOURIER_GROUPS = 4
FOURIER_GROUP_DIM = 128
FOURIER_WIDTH = N_FOURIER_GROUPS * FOURIER_GROUP_DIM
ROPE_DIM = QK_HEAD_DIM // 4
ROPE_THETA = 500000.0
D_FF = 4 * D_MODEL
PLE_DIM = 256
EPS = 1e-6

LANES = 128
VMEM_LIMIT = 56 * 1024 * 1024
F32 = jnp.float32
BF16 = jnp.bfloat16
LOG2E = 1.4426950408889634


def _rms(x, g):
    return x * lax.rsqrt(jnp.mean(x * x, axis=-1, keepdims=True) + EPS) * g


def _dot(a, b):
    return jnp.dot(a, b, preferred_element_type=F32)


def _sigmoid(x):
    return 1.0 / (1.0 + jnp.exp(-x))


def _const_spec(shape):
    nd = len(shape)
    return pl.BlockSpec(shape, lambda *_: (0,) * nd, pipeline_mode=pl.Buffered(1))


def _rope(y, c, sa, sb):
    outs = []
    for j in range(y.shape[1] // LANES):
        yc = y[:, j * LANES:(j + 1) * LANES]
        outs.append(yc * c + pltpu.roll(yc, ROPE_DIM // 2, 1) * sa
                    + pltpu.roll(yc, LANES - ROPE_DIM // 2, 1) * sb)
    return jnp.concatenate(outs, axis=1)


def _in_proj_kernel(x_ref, g_ref, w_ref, cq_ref, sqa_ref, sqb_ref, ck_ref, ska_ref, skb_ref,
                    fa_ref, fb_ref, q1_ref, q2_ref, k_ref, v_ref, gf_ref, ga_ref):
    h = _rms(x_ref[...], g_ref[...]).astype(BF16)
    o = 0
    fa_ref[...] = _dot(h, w_ref[:, o:o + FOURIER_WIDTH]).astype(BF16); o += FOURIER_WIDTH
    fb_ref[...] = _dot(h, w_ref[:, o:o + FOURIER_WIDTH]).astype(BF16); o += FOURIER_WIDTH
    q = _dot(h, w_ref[:, o:o + Q_WIDTH]); o += Q_WIDTH
    q = _rope(q, cq_ref[...], sqa_ref[...], sqb_ref[...])
    comp1 = lax.broadcasted_iota(jnp.int32, (1, Q_WIDTH), 1) % V_HEAD_DIM < QK_HEAD_DIM
    q1_ref[...] = jnp.where(comp1, q, 0.0).astype(BF16)
    q2_ref[...] = jnp.where(comp1, 0.0, q).astype(BF16)
    k = _dot(h, w_ref[:, o:o + Q_WIDTH]); o += Q_WIDTH
    k_ref[...] = _rope(k, ck_ref[...], ska_ref[...], skb_ref[...]).astype(BF16)
    v_ref[...] = _dot(h, w_ref[:, o:o + ATTN_WIDTH]).astype(BF16); o += ATTN_WIDTH
    gf_ref[...] = _sigmoid(_dot(h, w_ref[:, o:o + D_MODEL])).astype(BF16); o += D_MODEL
    ga_ref[...] = _sigmoid(_dot(h, w_ref[:, o:o + D_MODEL])).astype(BF16)


def _rope_tables(seq, scale):
    half = ROPE_DIM // 2
    pos = jnp.arange(seq, dtype=F32)
    inv_freq = ROPE_THETA ** (-(jnp.arange(0, ROPE_DIM, 2, dtype=F32) / ROPE_DIM))
    ang = pos[:, None] * inv_freq[None, :]
    cos, sin = jnp.cos(ang), jnp.sin(ang)
    ones = jnp.ones((seq, QK_HEAD_DIM - ROPE_DIM), F32)
    zeros = jnp.zeros((seq, QK_HEAD_DIM - ROPE_DIM), F32)
    zh = jnp.zeros((seq, half), F32)
    c = jnp.concatenate([cos, cos, ones], axis=1)
    sa = jnp.concatenate([zh, sin, zeros], axis=1)
    sb = jnp.concatenate([-sin, zh, zeros], axis=1)
    rep = LANES // QK_HEAD_DIM
    return tuple(jnp.tile(t * scale, (1, rep)) for t in (c, sa, sb))


def _in_proj(x2d, gain, w_in, seq, tm):
    n_tok = x2d.shape[0]
    q_scale = LOG2E / math.sqrt(QK_HEAD_DIM)
    tabs = _rope_tables(seq, q_scale) + _rope_tables(seq, 1.0)
    spt = seq // tm
    tok = lambda w: pl.BlockSpec((tm, w), lambda i: (i, 0))
    tab = pl.BlockSpec((tm, LANES), lambda i: (i % spt, 0))
    widths = (FOURIER_WIDTH, FOURIER_WIDTH, Q_WIDTH, Q_WIDTH, Q_WIDTH, ATTN_WIDTH, D_MODEL, D_MODEL)
    return pl.pallas_call(
        _in_proj_kernel,
        grid=(n_tok // tm,),
        in_specs=[tok(D_MODEL), _const_spec((1, D_MODEL)), _const_spec(w_in.shape)] + [tab] * 6,
        out_specs=[tok(w) for w in widths],
        out_shape=[jax.ShapeDtypeStruct((n_tok, w), BF16) for w in widths],
        compiler_params=pltpu.CompilerParams(dimension_semantics=("parallel",),
                                             vmem_limit_bytes=VMEM_LIMIT),
        name="in_proj",
    )(x2d, gain, w_in, *tabs)


def _chan_dft_weights():
    n = FOURIER_GROUP_DIM
    idx = np.arange(n)
    ang = 2.0 * np.pi * ((idx[:, None] * idx[None, :]) % n) / n
    eye = np.eye(N_FOURIER_GROUPS)
    c = np.kron(eye, np.cos(ang)) / math.sqrt(n)
    s = np.kron(eye, np.sin(ang)) / math.sqrt(n)
    return jnp.asarray(np.concatenate([c, s], axis=1), dtype=BF16)


def _fold_kernel(a_ref, b_ref, o_ref):
    o_ref[...] = _dot(a_ref[...], b_ref[...]).astype(BF16)


def _fold_chan_dft(w_f):
    return pl.pallas_call(
        _fold_kernel,
        out_shape=jax.ShapeDtypeStruct((w_f.shape[0], 2 * FOURIER_WIDTH), BF16),
        compiler_params=pltpu.CompilerParams(vmem_limit_bytes=VMEM_LIMIT),
        name="fold_chan_dft",
    )(w_f, _chan_dft_weights())


def _dense_dft_kernel(a_ref, b_ref, c_ref, s_ref, y_ref):
    y_ref[0] = (_dot(c_ref[...], a_ref[0]) + _dot(s_ref[...], b_ref[0])).astype(BF16)


def _seq_dft_dense(a, b, bsz, seq):
    j = jnp.arange(seq, dtype=jnp.int32)
    ang = ((j[:, None] * j[None, :]) % seq).astype(F32) * (2.0 * math.pi / seq)
    sc = 1.0 / math.sqrt(seq)
    c, s = (jnp.cos(ang) * sc).astype(BF16), (jnp.sin(ang) * (-sc)).astype(BF16)
    blk = pl.BlockSpec((1, seq, FOURIER_WIDTH), lambda bi: (bi, 0, 0))
    shp = (bsz, seq, FOURIER_WIDTH)
    y = pl.pallas_call(
        _dense_dft_kernel,
        grid=(bsz,),
        in_specs=[blk, blk, _const_spec(c.shape), _const_spec(s.shape)],
        out_specs=blk,
        out_shape=jax.ShapeDtypeStruct(shp, BF16),
        compiler_params=pltpu.CompilerParams(dimension_semantics=("parallel",),
                                             vmem_limit_bytes=VMEM_LIMIT),
        name="seq_dft_dense",
    )(a.reshape(shp), b.reshape(shp), c, s)
    return y.reshape(bsz * seq, FOURIER_WIDTH)


SEQ_DFT_N2 = 128
DENSE_DFT_MAX_SEQ = 2048


def _dft_stage1_kernel(a_ref, b_ref, m_ref, tc_ref, ts_ref, vr_ref, vi_ref):
    n1 = a_ref.shape[1]
    u = _dot(m_ref[...], jnp.concatenate([a_ref[0], b_ref[0]], axis=0))
    ur, ui = u[:n1], u[n1:]
    tc, ts = tc_ref[...], ts_ref[...]
    vr_ref[0] = (ur * tc - ui * ts).astype(BF16)
    vi_ref[0] = (ur * ts + ui * tc).astype(BF16)


def _dft_stage2_kernel(vr_ref, vi_ref, m_ref, y_ref):
    for i in range(vr_ref.shape[1]):
        v = jnp.concatenate([vr_ref[0, i], vi_ref[0, i]], axis=0)
        y_ref[0, :, i * FOURIER_WIDTH:(i + 1) * FOURIER_WIDTH] = _dot(m_ref[...], v).astype(BF16)


def _seq_dft(a, b, bsz, seq, tcols, group):
    n2 = SEQ_DFT_N2
    n1 = seq // n2
    cols = n2 * FOURIER_WIDTH
    j1 = np.arange(n1)
    ang1 = 2.0 * np.pi * ((j1[:, None] * j1[None, :]) % n1) / n1
    c1, s1 = np.cos(ang1), np.sin(ang1)
    m1 = jnp.asarray(np.block([[c1, -s1], [s1, c1]]) / math.sqrt(seq), dtype=BF16)
    j2 = np.arange(n2)
    ang2 = 2.0 * np.pi * ((j2[:, None] * j2[None, :]) % n2) / n2
    m3 = jnp.asarray(np.concatenate([np.cos(ang2), -np.sin(ang2)], axis=1), dtype=BF16)
    angt = (jnp.arange(n1, dtype=F32)[:, None] * jnp.arange(n2, dtype=F32)[None, :]) * (2.0 * math.pi / seq)
    tcos = jnp.repeat(jnp.cos(angt), FOURIER_WIDTH, axis=1)
    tsin = jnp.repeat(jnp.sin(angt), FOURIER_WIDTH, axis=1)

    blk = pl.BlockSpec((1, n1, tcols), lambda c, bi: (bi, 0, c))
    twd = pl.BlockSpec((n1, tcols), lambda c, bi: (0, c))
    vr, vi = pl.pallas_call(
        _dft_stage1_kernel,
        grid=(cols // tcols, bsz),
        in_specs=[blk, blk, _const_spec(m1.shape), twd, twd],
        out_specs=[blk, blk],
        out_shape=[jax.ShapeDtypeStruct((bsz, n1, cols), BF16)] * 2,
        compiler_params=pltpu.CompilerParams(dimension_semantics=("parallel", "parallel"),
                                             vmem_limit_bytes=VMEM_LIMIT),
        name="seq_dft_stage1",
    )(a.reshape(bsz, n1, cols), b.reshape(bsz, n1, cols), m1, tcos, tsin)

    vblk = pl.BlockSpec((1, group, n2, FOURIER_WIDTH), lambda bi, g: (bi, g, 0, 0))
    y = pl.pallas_call(
        _dft_stage2_kernel,
        grid=(bsz, n1 // group),
        in_specs=[vblk, vblk, _const_spec(m3.shape)],
        out_specs=pl.BlockSpec((1, n2, group * FOURIER_WIDTH), lambda bi, g: (bi, 0, g)),
        out_shape=jax.ShapeDtypeStruct((bsz, n2, n1 * FOURIER_WIDTH), BF16),
        compiler_params=pltpu.CompilerParams(dimension_semantics=("parallel", "parallel"),
                                             vmem_limit_bytes=VMEM_LIMIT),
        name="seq_dft_stage2",
    )(vr.reshape(bsz, n1, n2, FOURIER_WIDTH), vi.reshape(bsz, n1, n2, FOURIER_WIDTH), m3)
    return y.reshape(bsz * seq, FOURIER_WIDTH)


def _lambda_kernel(lam_init, q1_ref, k1_ref, q2_ref, k2_ref, o_ref):
    s1 = jnp.sum(q1_ref[...] * k1_ref[...], axis=-1, keepdims=True)
    s2 = jnp.sum(q2_ref[...] * k2_ref[...], axis=-1, keepdims=True)
    o_ref[...] = jnp.exp(s1) - jnp.exp(s2) + lam_init


def _lambda(lam_init, q1, k1, q2, k2):
    vec = lambda a: a.reshape(1, QK_HEAD_DIM).astype(F32)
    return pl.pallas_call(
        functools.partial(_lambda_kernel, lam_init),
        out_shape=jax.ShapeDtypeStruct((1, 1), F32),
        name="diff_lambda",
    )(vec(q1), vec(k1), vec(q2), vec(k2))


def _attn_kernel(tq, tk, unroll, fin_group, out_scale, lam_ref, q1_ref, q2_ref, k_ref, v_ref, sub_ref, o_ref,
                 s0_ref, s1_ref, p0_ref, p1_ref, a0_ref, a1_ref, m0_ref, m1_ref, acc_ref):
    s_refs, p_refs, a_refs, m_refs = (s0_ref, s1_ref), (p0_ref, p1_ref), (a0_ref, a1_ref), (m0_ref, m1_ref)
    seq = k_ref.shape[1]
    nq, nk = seq // tq, seq // tk
    n_steps = nq * nk
    assert nk & (nk - 1) == 0 and nq % fin_group == 0
    assert unroll % 2 == 0 and n_steps % unroll == 0
    nk_shift = nk.bit_length() - 1
    lam = lam_ref[0, 0]
    nt = (((1,), (1,)), ((), ()))
    ones = jnp.ones((tk, V_HEAD_DIM), BF16)

    def split(t):
        return lax.shift_right_logical(t, nk_shift), lax.bitwise_and(t, nk - 1)

    def scores(t, slot):
        qi, kj = split(t)
        rows = pl.ds(pl.multiple_of(qi * tq, tq), tq)
        qz = jnp.concatenate([q1_ref[0, rows, :], q2_ref[0, rows, :]], axis=0)
        kt = k_ref[0, pl.ds(pl.multiple_of(kj * tk, tk), tk), :]
        s_refs[slot][...] = lax.dot_general(qz, kt, nt, preferred_element_type=F32)

    def softmax(t, slot, can_be_first):
        _, kj = split(t)
        s_ref, m_in, m_out = s_refs[slot], m_refs[slot], m_refs[1 - slot]
        for g in range(2 * tq // 8):
            rows = slice(g * 8, (g + 1) * 8)
            blocks = [s_ref[rows, j * LANES:(j + 1) * LANES] for j in range(tk // LANES)]
            m_old = m_in[rows, :]
            if can_be_first:
                m_old = jnp.where(kj == 0, -jnp.inf, m_old)
            blk_max = functools.reduce(jnp.maximum, blocks)
            m_new = jnp.maximum(m_old, jnp.max(blk_max, axis=-1, keepdims=True))
            m_out[rows, :] = m_new
            a_refs[slot][rows, :] = jnp.exp2(m_old - m_new)
            p_refs[slot][rows, :] = jnp.concatenate(
                [jnp.exp2(b - m_new) for b in blocks], axis=1).astype(BF16)

    def pv(u, slot, tile, can_be_first):
        _, kj = split(u)
        vt = v_ref[0, pl.ds(pl.multiple_of(kj * tk, tk), tk), :]
        v1 = jnp.concatenate([vt, ones], axis=1)
        alpha = a_refs[slot][...]
        alpha2 = jnp.concatenate([alpha, alpha], axis=1)
        acc = acc_ref[tile]
        if can_be_first:
            acc = jnp.where(kj == 0, 0.0, acc)
        acc_ref[tile] = alpha2 * acc + _dot(p_refs[slot][...], v1)

    @pl.when((pl.program_id(0) == 0) & (pl.program_id(1) == 0))
    def _():
        acc_ref[...] = jnp.zeros(acc_ref.shape, F32)
        m0_ref[...] = jnp.zeros(m0_ref.shape, F32)

    scores(0, 0)

    def body(i, _):
        t = unroll * i
        for j in range(unroll):
            scores(jnp.minimum(t + j + 1, n_steps - 1), (j + 1) % 2)
            first = j % math.gcd(unroll, nk) == 0
            softmax(t + j, j % 2, first)
            pv(t + j, j % 2, split(t + j)[0], first)
        return 0

    lax.fori_loop(0, n_steps // unroll, body, 0)

    sub = sub_ref[...] * out_scale

    def finish(i, _):
        for j in range(fin_group):
            tile = i * fin_group + j
            a = acc_ref[tile]
            o = a[:, :V_HEAD_DIM] / a[:, V_HEAD_DIM:]
            att = o[:tq] - lam * o[tq:]
            o_ref[0, pl.ds(pl.multiple_of(tile * tq, tq), tq), :] = _rms(att, sub).astype(BF16)
        return 0

    lax.fori_loop(0, nq // fin_group, finish, 0)


def _diff_attn(lam, q1, q2, k, v, subln, out_scale, tq, tk, unroll, fin_group):
    b, seq, _ = k.shape
    head = pl.BlockSpec((1, seq, V_HEAD_DIM), lambda bi, h: (bi, 0, h))
    return pl.pallas_call(
        functools.partial(_attn_kernel, tq, tk, unroll, fin_group, out_scale),
        grid=(b, N_HEADS),
        in_specs=[pl.BlockSpec(memory_space=pltpu.SMEM), head, head, head, head,
                  _const_spec((1, V_HEAD_DIM))],
        out_specs=head,
        out_shape=jax.ShapeDtypeStruct((b, seq, ATTN_WIDTH), BF16),
        scratch_shapes=[pltpu.VMEM((2 * tq, tk), F32)] * 2 + [pltpu.VMEM((2 * tq, tk), BF16)] * 2
        + [pltpu.VMEM((2 * tq, LANES), F32)] * 4
        + [pltpu.VMEM((seq // tq, 2 * tq, 2 * V_HEAD_DIM), F32)],
        compiler_params=pltpu.CompilerParams(dimension_semantics=("arbitrary", "arbitrary"),
                                             vmem_limit_bytes=VMEM_LIMIT),
        name="diff_attn",
    )(lam, q1, q2, k, v, subln)


def _merge_kernel(x_ref, y_ref, att_ref, gf_ref, ga_ref, wf_ref, wa_ref, wo_ref, g_ref, o_ref):
    fo = _dot(y_ref[...], wf_ref[...])
    ao = _dot(att_ref[...], wa_ref[...])
    merged = gf_ref[...].astype(F32) * fo + ga_ref[...].astype(F32) * ao
    o_ref[...] = x_ref[...] + _rms(_dot(merged.astype(BF16), wo_ref[...]), g_ref[...])


def _merge(x2d, y2, att2d, gf, ga, w_fourier, w_attn, w_out, gain, tm):
    n_tok = x2d.shape[0]
    tok = lambda w: pl.BlockSpec((tm, w), lambda i: (i, 0))
    return pl.pallas_call(
        _merge_kernel,
        grid=(n_tok // tm,),
        in_specs=[tok(D_MODEL), tok(FOURIER_WIDTH), tok(ATTN_WIDTH), tok(D_MODEL), tok(D_MODEL),
                  _const_spec(w_fourier.shape), _const_spec(w_attn.shape),
                  _const_spec(w_out.shape), _const_spec((1, D_MODEL))],
        out_specs=tok(D_MODEL),
        out_shape=jax.ShapeDtypeStruct((n_tok, D_MODEL), F32),
        compiler_params=pltpu.CompilerParams(dimension_semantics=("parallel",),
                                             vmem_limit_bytes=VMEM_LIMIT),
        name="merge",
    )(x2d, y2, att2d, gf, ga, w_fourier, w_attn, w_out, gain)


def _mlp_ple_kernel(x_ref, p_ref, gpre_ref, wup_ref, wdn_ref, gpost_ref, wple_ref, wgate_ref,
                    gple_ref, o_ref):
    x = x_ref[...]
    h = _rms(x, gpre_ref[...]).astype(BF16)
    d = jnp.zeros(x.shape, F32)
    for c in range(D_FF // D_MODEL):
        cols = slice(c * D_MODEL, (c + 1) * D_MODEL)
        u = jnp.square(jnp.maximum(_dot(h, wup_ref[:, cols]), 0.0))
        d = d + _dot(u.astype(BF16), wdn_ref[cols, :])
    x = x + _rms(d, gpost_ref[...])
    e = _dot(p_ref[...].astype(BF16), wple_ref[...]) * _sigmoid(_dot(x.astype(BF16), wgate_ref[...]))
    o_ref[...] = x + _rms(e, gple_ref[...])


def _mlp_ple(x2d, p2d, g_pre, w_up, w_down, g_post, w_ple, w_gate, g_ple, tm):
    n_tok = x2d.shape[0]
    tok = lambda w: pl.BlockSpec((tm, w), lambda i: (i, 0))
    vec = _const_spec((1, D_MODEL))
    return pl.pallas_call(
        _mlp_ple_kernel,
        grid=(n_tok // tm,),
        in_specs=[tok(D_MODEL), tok(PLE_DIM), vec, _const_spec(w_up.shape),
                  _const_spec(w_down.shape), vec, _const_spec(w_ple.shape),
                  _const_spec(w_gate.shape), vec],
        out_specs=tok(D_MODEL),
        out_shape=jax.ShapeDtypeStruct((n_tok, D_MODEL), F32),
        compiler_params=pltpu.CompilerParams(dimension_semantics=("parallel",),
                                             vmem_limit_bytes=VMEM_LIMIT),
        name="mlp_ple",
    )(x2d, p2d, g_pre, w_up, w_down, g_post, w_ple, w_gate, g_ple)


def _layer(x, p, layer_idx, prm):
    (norm_mix_pre, w_in, w_fourier, w_attn, w_out, lq1, lk1, lq2, lk2, subln,
     norm_mix_post, norm_mlp_pre, w_up, w_down, norm_mlp_post, w_ple, w_gate, norm_ple_post) = prm
    b, seq, _ = x.shape
    n_tok = b * seq
    row = lambda g: g.reshape(1, -1).astype(F32)
    x2d = x.reshape(n_tok, D_MODEL)

    fa, fb, q1, q2, k, v, gf, ga = _in_proj(x2d, row(norm_mix_pre), w_in, seq, tm=512)
    if seq <= DENSE_DFT_MAX_SEQ:
        y2 = _seq_dft_dense(fa, fb, b, seq)
    else:
        y2 = _seq_dft(fa, fb, b, seq, tcols=(1 << 19) * SEQ_DFT_N2 // seq, group=8)

    lam_init = 0.8 - 0.6 * math.exp(-0.3 * layer_idx)
    lam = _lambda(lam_init, lq1, lk1, lq2, lk2)
    shp = (b, seq, Q_WIDTH)
    att = _diff_attn(lam, q1.reshape(shp), q2.reshape(shp), k.reshape(shp), v.reshape(shp), row(subln),
                     1.0 - lam_init, tq=256, tk=1024, unroll=8, fin_group=4)

    x1 = _merge(x2d, y2, att.reshape(n_tok, ATTN_WIDTH), gf, ga, w_fourier, w_attn, w_out,
                row(norm_mix_post), tm=512)
    out = _mlp_ple(x1, p.reshape(n_tok, PLE_DIM), row(norm_mlp_pre), w_up, w_down,
                   row(norm_mlp_post), w_ple, w_gate, row(norm_ple_post), tm=512)
    return out.reshape(b, seq, D_MODEL)


def _prepare_weights(w_in, w_fourier, w_attn, w_out, w_up, w_down, w_ple, w_gate):
    w_in = w_in.astype(BF16)
    w_in = jnp.concatenate([_fold_chan_dft(w_in[:, :FOURIER_WIDTH]), w_in[:, FOURIER_WIDTH:]], axis=1)
    return (w_in,) + tuple(w.astype(BF16) for w in (w_fourier, w_attn, w_out, w_up, w_down, w_ple, w_gate))


def kernel(x_prompt, x_sample, p_prompt, p_sample, norm_mix_pre, w_in, w_fourier, w_attn, w_out,
           lambda_q1, lambda_k1, lambda_q2, lambda_k2, subln, norm_mix_post, norm_mlp_pre,
           w_up, w_down, norm_mlp_post, w_ple, w_ple_gate, norm_ple_post):
    y_prompt, y_sample = x_prompt, x_sample
    for i in range(w_in.shape[0]):
        wi, wf, wa, wo, wu, wd, wp, wg = _prepare_weights(
            w_in[i], w_fourier[i], w_attn[i], w_out[i], w_up[i], w_down[i], w_ple[i], w_ple_gate[i])
        prm = (norm_mix_pre[i], wi, wf, wa, wo,
               lambda_q1[i], lambda_k1[i], lambda_q2[i], lambda_k2[i], subln[i],
               norm_mix_post[i], norm_mlp_pre[i], wu, wd, norm_mlp_post[i], wp, wg, norm_ple_post[i])
        y_prompt = _layer(y_prompt, p_prompt[i], i, prm)
        y_sample = _layer(y_sample, p_sample[i], i, prm)
    return (y_prompt, y_sample)
```

```python
import functools
import math

import jax
import jax.numpy as jnp
import numpy as np
from jax import lax
from jax.experimental import pallas as pl
from jax.experimental.pallas import tpu as pltpu

D_MODEL = 1024
N_HEADS = 8
QK_HEAD_DIM = 64
V_HEAD_DIM = 128
Q_WIDTH = N_HEADS * 2 * QK_HEAD_DIM
ATTN_WIDTH = N_HEADS * V_HEAD_DIM
N_FOURIER_GROUPS = 4
FOURIER_GROUP_DIM = 128
FOURIER_WIDTH = N_FOURIER_GROUPS * FOURIER_GROUP_DIM
ROPE_DIM = QK_HEAD_DIM // 4
ROPE_THETA = 500000.0
D_FF = 4 * D_MODEL
PLE_DIM = 256
EPS = 1e-6

LANES = 128
VMEM_LIMIT = 56 * 1024 * 1024
F32 = jnp.float32
BF16 = jnp.bfloat16
LOG2E = 1.4426950408889634


def _rms(x, g):
    return x * lax.rsqrt(jnp.mean(x * x, axis=-1, keepdims=True) + EPS) * g


def _dot(a, b):
    return jnp.dot(a, b, preferred_element_type=F32)


def _sigmoid(x):
    return 1.0 / (1.0 + jnp.exp(-x))


def _const_spec(shape):
    nd = len(shape)
    return pl.BlockSpec(shape, lambda *_: (0,) * nd, pipeline_mode=pl.Buffered(1))


def _rope(y, c, sa, sb):
    outs = []
    for j in range(y.shape[1] // LANES):
        yc = y[:, j * LANES:(j + 1) * LANES]
        outs.append(yc * c + pltpu.roll(yc, ROPE_DIM // 2, 1) * sa
                    + pltpu.roll(yc, LANES - ROPE_DIM // 2, 1) * sb)
    return jnp.concatenate(outs, axis=1)


def _in_proj_kernel(x_ref, g_ref, w_ref, cq_ref, sqa_ref, sqb_ref, ck_ref, ska_ref, skb_ref,
                    fa_ref, fb_ref, q1_ref, q2_ref, k_ref, v_ref, gf_ref, ga_ref):
    h = _rms(x_ref[...], g_ref[...]).astype(BF16)
    o = 0
    fa_ref[...] = _dot(h, w_ref[:, o:o + FOURIER_WIDTH]).astype(BF16); o += FOURIER_WIDTH
    fb_ref[...] = _dot(h, w_ref[:, o:o + FOURIER_WIDTH]).astype(BF16); o += FOURIER_WIDTH
    q = _dot(h, w_ref[:, o:o + Q_WIDTH]); o += Q_WIDTH
    q = _rope(q, cq_ref[...], sqa_ref[...], sqb_ref[...])
    comp1 = lax.broadcasted_iota(jnp.int32, (1, Q_WIDTH), 1) % V_HEAD_DIM < QK_HEAD_DIM
    q1_ref[...] = jnp.where(comp1, q, 0.0).astype(BF16)
    q2_ref[...] = jnp.where(comp1, 0.0, q).astype(BF16)
    k = _dot(h, w_ref[:, o:o + Q_WIDTH]); o += Q_WIDTH
    k_ref[...] = _rope(k, ck_ref[...], ska_ref[...], skb_ref[...]).astype(BF16)
    v_ref[...] = _dot(h, w_ref[:, o:o + ATTN_WIDTH]).astype(BF16); o += ATTN_WIDTH
    gf_ref[...] = _sigmoid(_dot(h, w_ref[:, o:o + D_MODEL])).astype(BF16); o += D_MODEL
    ga_ref[...] = _sigmoid(_dot(h, w_ref[:, o:o + D_MODEL])).astype(BF16)


def _rope_tables(seq, scale):
    half = ROPE_DIM // 2
    pos = jnp.arange(seq, dtype=F32)
    inv_freq = ROPE_THETA ** (-(jnp.arange(0, ROPE_DIM, 2, dtype=F32) / ROPE_DIM))
    ang = pos[:, None] * inv_freq[None, :]
    cos, sin = jnp.cos(ang), jnp.sin(ang)
    ones = jnp.ones((seq, QK_HEAD_DIM - ROPE_DIM), F32)
    zeros = jnp.zeros((seq, QK_HEAD_DIM - ROPE_DIM), F32)
    zh = jnp.zeros((seq, half), F32)
    c = jnp.concatenate([cos, cos, ones], axis=1)
    sa = jnp.concatenate([zh, sin, zeros], axis=1)
    sb = jnp.concatenate([-sin, zh, zeros], axis=1)
    rep = LANES // QK_HEAD_DIM
    return tuple(jnp.tile(t * scale, (1, rep)) for t in (c, sa, sb))


def _in_proj(x2d, gain, w_in, seq, tm):
    n_tok = x2d.shape[0]
    q_scale = LOG2E / math.sqrt(QK_HEAD_DIM)
    tabs = _rope_tables(seq, q_scale) + _rope_tables(seq, 1.0)
    spt = seq // tm
    tok = lambda w: pl.BlockSpec((tm, w), lambda i: (i, 0))
    tab = pl.BlockSpec((tm, LANES), lambda i: (i % spt, 0))
    widths = (FOURIER_WIDTH, FOURIER_WIDTH, Q_WIDTH, Q_WIDTH, Q_WIDTH, ATTN_WIDTH, D_MODEL, D_MODEL)
    return pl.pallas_call(
        _in_proj_kernel,
        grid=(n_tok // tm,),
        in_specs=[tok(D_MODEL), _const_spec((1, D_MODEL)), _const_spec(w_in.shape)] + [tab] * 6,
        out_specs=[tok(w) for w in widths],
        out_shape=[jax.ShapeDtypeStruct((n_tok, w), BF16) for w in widths],
        compiler_params=pltpu.CompilerParams(dimension_semantics=("parallel",),
                                             vmem_limit_bytes=VMEM_LIMIT),
        name="in_proj",
    )(x2d, gain, w_in, *tabs)


def _chan_dft_weights():
    n = FOURIER_GROUP_DIM
    idx = np.arange(n)
    ang = 2.0 * np.pi * ((idx[:, None] * idx[None, :]) % n) / n
    eye = np.eye(N_FOURIER_GROUPS)
    c = np.kron(eye, np.cos(ang)) / math.sqrt(n)
    s = np.kron(eye, np.sin(ang)) / math.sqrt(n)
    return jnp.asarray(np.concatenate([c, s], axis=1), dtype=BF16)


def _fold_kernel(a_ref, b_ref, o_ref):
    o_ref[...] = _dot(a_ref[...], b_ref[...]).astype(BF16)


def _fold_chan_dft(w_f):
    return pl.pallas_call(
        _fold_kernel,
        out_shape=jax.ShapeDtypeStruct((w_f.shape[0], 2 * FOURIER_WIDTH), BF16),
        compiler_params=pltpu.CompilerParams(vmem_limit_bytes=VMEM_LIMIT),
        name="fold_chan_dft",
    )(w_f, _chan_dft_weights())


def _dense_dft_kernel(a_ref, b_ref, c_ref, s_ref, y_ref):
    y_ref[0] = (_dot(c_ref[...], a_ref[0]) + _dot(s_ref[...], b_ref[0])).astype(BF16)


def _seq_dft_dense(a, b, bsz, seq):
    lo_n = 64
    k = jnp.arange(seq, dtype=jnp.int32)[None, :]

    def cos_sin(rows):
        ang = ((rows[:, None] * k) % seq).astype(F32) * (2.0 * math.pi / seq)
        return jnp.cos(ang), jnp.sin(ang)

    (ch, sh), (cl, sl) = cos_sin(jnp.arange(seq // lo_n, dtype=jnp.int32) * lo_n), cos_sin(
        jnp.arange(lo_n, dtype=jnp.int32))
    sc = 1.0 / math.sqrt(seq)
    c = ((ch[:, None] * cl[None] - sh[:, None] * sl[None]) * sc).astype(BF16).reshape(seq, seq)
    s = ((sh[:, None] * cl[None] + ch[:, None] * sl[None]) * (-sc)).astype(BF16).reshape(seq, seq)
    blk = pl.BlockSpec((1, seq, FOURIER_WIDTH), lambda bi: (bi, 0, 0))
    shp = (bsz, seq, FOURIER_WIDTH)
    y = pl.pallas_call(
        _dense_dft_kernel,
        grid=(bsz,),
        in_specs=[blk, blk, _const_spec(c.shape), _const_spec(s.shape)],
        out_specs=blk,
        out_shape=jax.ShapeDtypeStruct(shp, BF16),
        compiler_params=pltpu.CompilerParams(dimension_semantics=("parallel",),
                                             vmem_limit_bytes=VMEM_LIMIT),
        name="seq_dft_dense",
    )(a.reshape(shp), b.reshape(shp), c, s)
    return y.reshape(bsz * seq, FOURIER_WIDTH)


SEQ_DFT_N2 = 128
DENSE_DFT_MAX_SEQ = 2048


def _dft_stage1_kernel(a_ref, b_ref, m_ref, tc_ref, ts_ref, vr_ref, vi_ref):
    n1 = a_ref.shape[1]
    u = _dot(m_ref[...], jnp.concatenate([a_ref[0], b_ref[0]], axis=0))
    ur, ui = u[:n1], u[n1:]
    tc, ts = tc_ref[...], ts_ref[...]
    vr_ref[0] = (ur * tc - ui * ts).astype(BF16)
    vi_ref[0] = (ur * ts + ui * tc).astype(BF16)


def _dft_stage2_kernel(vr_ref, vi_ref, m_ref, y_ref):
    for i in range(vr_ref.shape[1]):
        v = jnp.concatenate([vr_ref[0, i], vi_ref[0, i]], axis=0)
        y_ref[0, :, i * FOURIER_WIDTH:(i + 1) * FOURIER_WIDTH] = _dot(m_ref[...], v).astype(BF16)


def _seq_dft(a, b, bsz, seq, tcols, group):
    n2 = SEQ_DFT_N2
    n1 = seq // n2
    cols = n2 * FOURIER_WIDTH
    j1 = np.arange(n1)
    ang1 = 2.0 * np.pi * ((j1[:, None] * j1[None, :]) % n1) / n1
    c1, s1 = np.cos(ang1), np.sin(ang1)
    m1 = jnp.asarray(np.block([[c1, -s1], [s1, c1]]) / math.sqrt(seq), dtype=BF16)
    j2 = np.arange(n2)
    ang2 = 2.0 * np.pi * ((j2[:, None] * j2[None, :]) % n2) / n2
    m3 = jnp.asarray(np.concatenate([np.cos(ang2), -np.sin(ang2)], axis=1), dtype=BF16)
    angt = (jnp.arange(n1, dtype=F32)[:, None] * jnp.arange(n2, dtype=F32)[None, :]) * (2.0 * math.pi / seq)
    tcos = jnp.repeat(jnp.cos(angt), FOURIER_WIDTH, axis=1)
    tsin = jnp.repeat(jnp.sin(angt), FOURIER_WIDTH, axis=1)

    blk = pl.BlockSpec((1, n1, tcols), lambda c, bi: (bi, 0, c))
    twd = pl.BlockSpec((n1, tcols), lambda c, bi: (0, c))
    vr, vi = pl.pallas_call(
        _dft_stage1_kernel,
        grid=(cols // tcols, bsz),
        in_specs=[blk, blk, _const_spec(m1.shape), twd, twd],
        out_specs=[blk, blk],
        out_shape=[jax.ShapeDtypeStruct((bsz, n1, cols), BF16)] * 2,
        compiler_params=pltpu.CompilerParams(dimension_semantics=("parallel", "parallel"),
                                             vmem_limit_bytes=VMEM_LIMIT),
        name="seq_dft_stage1",
    )(a.reshape(bsz, n1, cols), b.reshape(bsz, n1, cols), m1, tcos, tsin)

    vblk = pl.BlockSpec((1, group, n2, FOURIER_WIDTH), lambda bi, g: (bi, g, 0, 0))
    y = pl.pallas_call(
        _dft_stage2_kernel,
        grid=(bsz, n1 // group),
        in_specs=[vblk, vblk, _const_spec(m3.shape)],
        out_specs=pl.BlockSpec((1, n2, group * FOURIER_WIDTH), lambda bi, g: (bi, 0, g)),
        out_shape=jax.ShapeDtypeStruct((bsz, n2, n1 * FOURIER_WIDTH), BF16),
        compiler_params=pltpu.CompilerParams(dimension_semantics=("parallel", "parallel"),
                                             vmem_limit_bytes=VMEM_LIMIT),
        name="seq_dft_stage2",
    )(vr.reshape(bsz, n1, n2, FOURIER_WIDTH), vi.reshape(bsz, n1, n2, FOURIER_WIDTH), m3)
    return y.reshape(bsz * seq, FOURIER_WIDTH)


def _lambda_kernel(lam_init, q1_ref, k1_ref, q2_ref, k2_ref, o_ref):
    s1 = jnp.sum(q1_ref[...] * k1_ref[...], axis=-1, keepdims=True)
    s2 = jnp.sum(q2_ref[...] * k2_ref[...], axis=-1, keepdims=True)
    o_ref[...] = jnp.exp(s1) - jnp.exp(s2) + lam_init


def _lambda(lam_init, q1, k1, q2, k2):
    vec = lambda a: a.reshape(1, QK_HEAD_DIM).astype(F32)
    return pl.pallas_call(
        functools.partial(_lambda_kernel, lam_init),
        out_shape=jax.ShapeDtypeStruct((1, 1), F32),
        name="diff_lambda",
    )(vec(q1), vec(k1), vec(q2), vec(k2))


def _attn_kernel(tq, tk, unroll, fin_group, out_scale, lam_ref, q1_ref, q2_ref, k_ref, v_ref, sub_ref, o_ref,
                 s0_ref, s1_ref, p0_ref, p1_ref, a0_ref, a1_ref, m0_ref, m1_ref, acc_ref):
    s_refs, p_refs, a_refs, m_refs = (s0_ref, s1_ref), (p0_ref, p1_ref), (a0_ref, a1_ref), (m0_ref, m1_ref)
    seq = k_ref.shape[1]
    nq, nk = seq // tq, seq // tk
    n_steps = nq * nk
    assert nk & (nk - 1) == 0 and nq % fin_group == 0
    assert unroll % 2 == 0 and n_steps % unroll == 0
    nk_shift = nk.bit_length() - 1
    lam = lam_ref[0, 0]
    nt = (((1,), (1,)), ((), ()))
    ones = jnp.ones((tk, V_HEAD_DIM), BF16)

    def split(t):
        return lax.shift_right_logical(t, nk_shift), lax.bitwise_and(t, nk - 1)

    def scores(t, slot):
        qi, kj = split(t)
        rows = pl.ds(pl.multiple_of(qi * tq, tq), tq)
        qz = jnp.concatenate([q1_ref[0, rows, :], q2_ref[0, rows, :]], axis=0)
        kt = k_ref[0, pl.ds(pl.multiple_of(kj * tk, tk), tk), :]
        s_refs[slot][...] = lax.dot_general(qz, kt, nt, preferred_element_type=F32)

    def softmax(t, slot, can_be_first):
        _, kj = split(t)
        s_ref, m_in, m_out = s_refs[slot], m_refs[slot], m_refs[1 - slot]
        for g in range(2 * tq // 8):
            rows = slice(g * 8, (g + 1) * 8)
            blocks = [s_ref[rows, j * LANES:(j + 1) * LANES] for j in range(tk // LANES)]
            m_old = m_in[rows, :]
            if can_be_first:
                m_old = jnp.where(kj == 0, -jnp.inf, m_old)
            blk_max = functools.reduce(jnp.maximum, blocks)
            m_new = jnp.maximum(m_old, jnp.max(blk_max, axis=-1, keepdims=True))
            m_out[rows, :] = m_new
            a_refs[slot][rows, :] = jnp.exp2(m_old - m_new)
            p_refs[slot][rows, :] = jnp.concatenate(
                [jnp.exp2(b - m_new) for b in blocks], axis=1).astype(BF16)

    def pv(u, slot, tile, can_be_first):
        _, kj = split(u)
        vt = v_ref[0, pl.ds(pl.multiple_of(kj * tk, tk), tk), :]
        v1 = jnp.concatenate([vt, ones], axis=1)
        alpha = a_refs[slot][...]
        alpha2 = jnp.concatenate([alpha, alpha], axis=1)
        acc = acc_ref[tile]
        if can_be_first:
            acc = jnp.where(kj == 0, 0.0, acc)
        acc_ref[tile] = alpha2 * acc + _dot(p_refs[slot][...], v1)

    @pl.when((pl.program_id(0) == 0) & (pl.program_id(1) == 0))
    def _():
        acc_ref[...] = jnp.zeros(acc_ref.shape, F32)
        m0_ref[...] = jnp.zeros(m0_ref.shape, F32)

    scores(0, 0)

    def body(i, _):
        t = unroll * i
        for j in range(unroll):
            scores(jnp.minimum(t + j + 1, n_steps - 1), (j + 1) % 2)
            first = j % math.gcd(unroll, nk) == 0
            softmax(t + j, j % 2, first)
            pv(t + j, j % 2, split(t + j)[0], first)
        return 0

    lax.fori_loop(0, n_steps // unroll, body, 0)

    sub = sub_ref[...] * out_scale

    def finish(i, _):
        for j in range(fin_group):
            tile = i * fin_group + j
            a = acc_ref[tile]
            o = a[:, :V_HEAD_DIM] / a[:, V_HEAD_DIM:]
            att = o[:tq] - lam * o[tq:]
            o_ref[0, pl.ds(pl.multiple_of(tile * tq, tq), tq), :] = _rms(att, sub).astype(BF16)
        return 0

    lax.fori_loop(0, nq // fin_group, finish, 0)


def _diff_attn(lam, q1, q2, k, v, subln, out_scale, tq, tk, unroll, fin_group):
    b, seq, _ = k.shape
    head = pl.BlockSpec((1, seq, V_HEAD_DIM), lambda bi, h: (bi, 0, h))
    return pl.pallas_call(
        functools.partial(_attn_kernel, tq, tk, unroll, fin_group, out_scale),
        grid=(b, N_HEADS),
        in_specs=[pl.BlockSpec(memory_space=pltpu.SMEM), head, head, head, head,
                  _const_spec((1, V_HEAD_DIM))],
        out_specs=head,
        out_shape=jax.ShapeDtypeStruct((b, seq, ATTN_WIDTH), BF16),
        scratch_shapes=[pltpu.VMEM((2 * tq, tk), F32)] * 2 + [pltpu.VMEM((2 * tq, tk), BF16)] * 2
        + [pltpu.VMEM((2 * tq, LANES), F32)] * 4
        + [pltpu.VMEM((seq // tq, 2 * tq, 2 * V_HEAD_DIM), F32)],
        compiler_params=pltpu.CompilerParams(dimension_semantics=("arbitrary", "arbitrary"),
                                             vmem_limit_bytes=VMEM_LIMIT),
        name="diff_attn",
    )(lam, q1, q2, k, v, subln)


def _merge_kernel(x_ref, y_ref, att_ref, gf_ref, ga_ref, wf_ref, wa_ref, wo_ref, g_ref, o_ref):
    fo = _dot(y_ref[...], wf_ref[...])
    ao = _dot(att_ref[...], wa_ref[...])
    merged = gf_ref[...].astype(F32) * fo + ga_ref[...].astype(F32) * ao
    o_ref[...] = x_ref[...] + _rms(_dot(merged.astype(BF16), wo_ref[...]), g_ref[...])


def _merge(x2d, y2, att2d, gf, ga, w_fourier, w_attn, w_out, gain, tm):
    n_tok = x2d.shape[0]
    tok = lambda w: pl.BlockSpec((tm, w), lambda i: (i, 0))
    return pl.pallas_call(
        _merge_kernel,
        grid=(n_tok // tm,),
        in_specs=[tok(D_MODEL), tok(FOURIER_WIDTH), tok(ATTN_WIDTH), tok(D_MODEL), tok(D_MODEL),
                  _const_spec(w_fourier.shape), _const_spec(w_attn.shape),
                  _const_spec(w_out.shape), _const_spec((1, D_MODEL))],
        out_specs=tok(D_MODEL),
        out_shape=jax.ShapeDtypeStruct((n_tok, D_MODEL), F32),
        compiler_params=pltpu.CompilerParams(dimension_semantics=("parallel",),
                                             vmem_limit_bytes=VMEM_LIMIT),
        name="merge",
    )(x2d, y2, att2d, gf, ga, w_fourier, w_attn, w_out, gain)


def _mlp_ple_kernel(x_ref, p_ref, gpre_ref, wup_ref, wdn_ref, gpost_ref, wple_ref, wgate_ref,
                    gple_ref, o_ref):
    x = x_ref[...]
    h = _rms(x, gpre_ref[...]).astype(BF16)
    d = jnp.zeros(x.shape, F32)
    for c in range(D_FF // D_MODEL):
        cols = slice(c * D_MODEL, (c + 1) * D_MODEL)
        u = jnp.square(jnp.maximum(_dot(h, wup_ref[:, cols]), 0.0))
        d = d + _dot(u.astype(BF16), wdn_ref[cols, :])
    x = x + _rms(d, gpost_ref[...])
    e = _dot(p_ref[...].astype(BF16), wple_ref[...]) * _sigmoid(_dot(x.astype(BF16), wgate_ref[...]))
    o_ref[...] = x + _rms(e, gple_ref[...])


def _mlp_ple(x2d, p2d, g_pre, w_up, w_down, g_post, w_ple, w_gate, g_ple, tm):
    n_tok = x2d.shape[0]
    tok = lambda w: pl.BlockSpec((tm, w), lambda i: (i, 0))
    vec = _const_spec((1, D_MODEL))
    return pl.pallas_call(
        _mlp_ple_kernel,
        grid=(n_tok // tm,),
        in_specs=[tok(D_MODEL), tok(PLE_DIM), vec, _const_spec(w_up.shape),
                  _const_spec(w_down.shape), vec, _const_spec(w_ple.shape),
                  _const_spec(w_gate.shape), vec],
        out_specs=tok(D_MODEL),
        out_shape=jax.ShapeDtypeStruct((n_tok, D_MODEL), F32),
        compiler_params=pltpu.CompilerParams(dimension_semantics=("parallel",),
                                             vmem_limit_bytes=VMEM_LIMIT),
        name="mlp_ple",
    )(x2d, p2d, g_pre, w_up, w_down, g_post, w_ple, w_gate, g_ple)


def _layer(x, p, layer_idx, prm):
    (norm_mix_pre, w_in, w_fourier, w_attn, w_out, lq1, lk1, lq2, lk2, subln,
     norm_mix_post, norm_mlp_pre, w_up, w_down, norm_mlp_post, w_ple, w_gate, norm_ple_post) = prm
    b, seq, _ = x.shape
    n_tok = b * seq
    row = lambda g: g.reshape(1, -1).astype(F32)
    x2d = x.reshape(n_tok, D_MODEL)

    fa, fb, q1, q2, k, v, gf, ga = _in_proj(x2d, row(norm_mix_pre), w_in, seq, tm=512)
    if seq <= DENSE_DFT_MAX_SEQ:
        y2 = _seq_dft_dense(fa, fb, b, seq)
    else:
        y2 = _seq_dft(fa, fb, b, seq, tcols=(1 << 19) * SEQ_DFT_N2 // seq, group=8)

    lam_init = 0.8 - 0.6 * math.exp(-0.3 * layer_idx)
    lam = _lambda(lam_init, lq1, lk1, lq2, lk2)
    shp = (b, seq, Q_WIDTH)
    att = _diff_attn(lam, q1.reshape(shp), q2.reshape(shp), k.reshape(shp), v.reshape(shp), row(subln),
                     1.0 - lam_init, tq=256, tk=1024, unroll=8, fin_group=4)

    x1 = _merge(x2d, y2, att.reshape(n_tok, ATTN_WIDTH), gf, ga, w_fourier, w_attn, w_out,
                row(norm_mix_post), tm=1024)
    out = _mlp_ple(x1, p.reshape(n_tok, PLE_DIM), row(norm_mlp_pre), w_up, w_down,
                   row(norm_mlp_post), w_ple, w_gate, row(norm_ple_post), tm=1024)
    return out.reshape(b, seq, D_MODEL)


def _prepare_weights(w_in, w_fourier, w_attn, w_out, w_up, w_down, w_ple, w_gate):
    w_in = w_in.astype(BF16)
    w_in = jnp.concatenate([_fold_chan_dft(w_in[:, :FOURIER_WIDTH]), w_in[:, FOURIER_WIDTH:]], axis=1)
    return (w_in,) + tuple(w.astype(BF16) for w in (w_fourier, w_attn, w_out, w_up, w_down, w_ple, w_gate))


def kernel(x_prompt, x_sample, p_prompt, p_sample, norm_mix_pre, w_in, w_fourier, w_attn, w_out,
           lambda_q1, lambda_k1, lambda_q2, lambda_k2, subln, norm_mix_post, norm_mlp_pre,
           w_up, w_down, norm_mlp_post, w_ple, w_ple_gate, norm_ple_post):
    y_prompt, y_sample = x_prompt, x_sample
    for i in range(w_in.shape[0]):
        wi, wf, wa, wo, wu, wd, wp, wg = _prepare_weights(
            w_in[i], w_fourier[i], w_attn[i], w_out[i], w_up[i], w_down[i], w_ple[i], w_ple_gate[i])
        prm = (norm_mix_pre[i], wi, wf, wa, wo,
               lambda_q1[i], lambda_k1[i], lambda_q2[i], lambda_k2[i], subln[i],
               norm_mix_post[i], norm_mlp_pre[i], wu, wd, norm_mlp_post[i], wp, wg, norm_ple_post[i])
        y_prompt = _layer(y_prompt, p_prompt[i], i, prm)
        y_sample = _layer(y_sample, p_sample[i], i, prm)
    return (y_prompt, y_sample)
```

```python
import functools
import math

import jax
import jax.numpy as jnp
import numpy as np
from jax import lax
from jax.experimental import pallas as pl
from jax.experimental.pallas import tpu as pltpu

D_MODEL = 1024
N_HEADS = 8
QK_HEAD_DIM = 64
V_HEAD_DIM = 128
Q_WIDTH = N_HEADS * 2 * QK_HEAD_DIM
ATTN_WIDTH = N_HEADS * V_HEAD_DIM
N_FOURIER_GROUPS = 4
FOURIER_GROUP_DIM = 128
FOURIER_WIDTH = N_FOURIER_GROUPS * FOURIER_GROUP_DIM
ROPE_DIM = QK_HEAD_DIM // 4
ROPE_THETA = 500000.0
D_FF = 4 * D_MODEL
PLE_DIM = 256
EPS = 1e-6

LANES = 128
VMEM_LIMIT = 56 * 1024 * 1024
F32 = jnp.float32
BF16 = jnp.bfloat16
LOG2E = 1.4426950408889634


def _rms(x, g):
    return x * lax.rsqrt(jnp.mean(x * x, axis=-1, keepdims=True) + EPS) * g


def _dot(a, b):
    return jnp.dot(a, b, preferred_element_type=F32)


def _sigmoid(x):
    return 1.0 / (1.0 + jnp.exp(-x))


def _const_spec(shape):
    nd = len(shape)
    return pl.BlockSpec(shape, lambda *_: (0,) * nd, pipeline_mode=pl.Buffered(1))


def _rope(y, c, sa, sb):
    outs = []
    for j in range(y.shape[1] // LANES):
        yc = y[:, j * LANES:(j + 1) * LANES]
        outs.append(yc * c + pltpu.roll(yc, ROPE_DIM // 2, 1) * sa
                    + pltpu.roll(yc, LANES - ROPE_DIM // 2, 1) * sb)
    return jnp.concatenate(outs, axis=1)


def _store_heads(ref, val):
    for hd in range(N_HEADS):
        ref[0, hd] = val[:, hd * V_HEAD_DIM:(hd + 1) * V_HEAD_DIM]


def _in_proj_kernel(x_ref, g_ref, w_ref, cq_ref, sqa_ref, sqb_ref, ck_ref, ska_ref, skb_ref,
                    fa_ref, fb_ref, q1_ref, q2_ref, k_ref, v_ref, gf_ref, ga_ref):
    h = _rms(x_ref[...], g_ref[...]).astype(BF16)
    o = 0
    fa_ref[...] = _dot(h, w_ref[:, o:o + FOURIER_WIDTH]).astype(BF16); o += FOURIER_WIDTH
    fb_ref[...] = _dot(h, w_ref[:, o:o + FOURIER_WIDTH]).astype(BF16); o += FOURIER_WIDTH
    q = _dot(h, w_ref[:, o:o + Q_WIDTH]); o += Q_WIDTH
    q = _rope(q, cq_ref[...], sqa_ref[...], sqb_ref[...])
    comp1 = lax.broadcasted_iota(jnp.int32, (1, Q_WIDTH), 1) % V_HEAD_DIM < QK_HEAD_DIM
    _store_heads(q1_ref, jnp.where(comp1, q, 0.0).astype(BF16))
    _store_heads(q2_ref, jnp.where(comp1, 0.0, q).astype(BF16))
    k = _dot(h, w_ref[:, o:o + Q_WIDTH]); o += Q_WIDTH
    _store_heads(k_ref, _rope(k, ck_ref[...], ska_ref[...], skb_ref[...]).astype(BF16))
    _store_heads(v_ref, _dot(h, w_ref[:, o:o + ATTN_WIDTH]).astype(BF16)); o += ATTN_WIDTH
    gf_ref[...] = _sigmoid(_dot(h, w_ref[:, o:o + D_MODEL])).astype(BF16); o += D_MODEL
    ga_ref[...] = _sigmoid(_dot(h, w_ref[:, o:o + D_MODEL])).astype(BF16)


def _rope_tables(seq, scale):
    half = ROPE_DIM // 2
    pos = jnp.arange(seq, dtype=F32)
    inv_freq = ROPE_THETA ** (-(jnp.arange(0, ROPE_DIM, 2, dtype=F32) / ROPE_DIM))
    ang = pos[:, None] * inv_freq[None, :]
    cos, sin = jnp.cos(ang), jnp.sin(ang)
    ones = jnp.ones((seq, QK_HEAD_DIM - ROPE_DIM), F32)
    zeros = jnp.zeros((seq, QK_HEAD_DIM - ROPE_DIM), F32)
    zh = jnp.zeros((seq, half), F32)
    c = jnp.concatenate([cos, cos, ones], axis=1)
    sa = jnp.concatenate([zh, sin, zeros], axis=1)
    sb = jnp.concatenate([-sin, zh, zeros], axis=1)
    rep = LANES // QK_HEAD_DIM
    return tuple(jnp.tile(t * scale, (1, rep)) for t in (c, sa, sb))


def _in_proj(x2d, gain, w_in, seq, tm):
    n_tok = x2d.shape[0]
    q_scale = LOG2E / math.sqrt(QK_HEAD_DIM)
    tabs = _rope_tables(seq, q_scale) + _rope_tables(seq, 1.0)
    spt = seq // tm
    tok = lambda w: pl.BlockSpec((tm, w), lambda i: (i, 0))
    tab = pl.BlockSpec((tm, LANES), lambda i: (i % spt, 0))
    heads = pl.BlockSpec((1, N_HEADS, tm, V_HEAD_DIM), lambda i: (i // spt, 0, i % spt, 0))
    heads_shape = jax.ShapeDtypeStruct((n_tok // seq, N_HEADS, seq, V_HEAD_DIM), BF16)
    flat = lambda w: (tok(w), jax.ShapeDtypeStruct((n_tok, w), BF16))
    outs = [flat(FOURIER_WIDTH)] * 2 + [(heads, heads_shape)] * 4 + [flat(D_MODEL)] * 2
    return pl.pallas_call(
        _in_proj_kernel,
        grid=(n_tok // tm,),
        in_specs=[tok(D_MODEL), _const_spec((1, D_MODEL)), _const_spec(w_in.shape)] + [tab] * 6,
        out_specs=[spec for spec, _ in outs],
        out_shape=[shape for _, shape in outs],
        compiler_params=pltpu.CompilerParams(dimension_semantics=("parallel",),
                                             vmem_limit_bytes=VMEM_LIMIT),
        name="in_proj",
    )(x2d, gain, w_in, *tabs)


def _chan_dft_weights():
    n = FOURIER_GROUP_DIM
    idx = np.arange(n)
    ang = 2.0 * np.pi * ((idx[:, None] * idx[None, :]) % n) / n
    eye = np.eye(N_FOURIER_GROUPS)
    c = np.kron(eye, np.cos(ang)) / math.sqrt(n)
    s = np.kron(eye, np.sin(ang)) / math.sqrt(n)
    return jnp.asarray(np.concatenate([c, s], axis=1), dtype=BF16)


def _fold_kernel(a_ref, b_ref, o_ref):
    o_ref[...] = _dot(a_ref[...], b_ref[...]).astype(BF16)


def _fold_chan_dft(w_f):
    return pl.pallas_call(
        _fold_kernel,
        out_shape=jax.ShapeDtypeStruct((w_f.shape[0], 2 * FOURIER_WIDTH), BF16),
        compiler_params=pltpu.CompilerParams(vmem_limit_bytes=VMEM_LIMIT),
        name="fold_chan_dft",
    )(w_f, _chan_dft_weights())


def _dense_dft_kernel(a_ref, b_ref, c_ref, s_ref, y_ref):
    y_ref[0] = (_dot(c_ref[...], a_ref[0]) + _dot(s_ref[...], b_ref[0])).astype(BF16)


def _seq_dft_dense(a, b, bsz, seq):
    lo_n = 64
    k = jnp.arange(seq, dtype=jnp.int32)[None, :]

    def cos_sin(rows):
        ang = ((rows[:, None] * k) % seq).astype(F32) * (2.0 * math.pi / seq)
        return jnp.cos(ang), jnp.sin(ang)

    (ch, sh), (cl, sl) = cos_sin(jnp.arange(seq // lo_n, dtype=jnp.int32) * lo_n), cos_sin(
        jnp.arange(lo_n, dtype=jnp.int32))
    sc = 1.0 / math.sqrt(seq)
    c = ((ch[:, None] * cl[None] - sh[:, None] * sl[None]) * sc).astype(BF16).reshape(seq, seq)
    s = ((sh[:, None] * cl[None] + ch[:, None] * sl[None]) * (-sc)).astype(BF16).reshape(seq, seq)
    blk = pl.BlockSpec((1, seq, FOURIER_WIDTH), lambda bi: (bi, 0, 0))
    shp = (bsz, seq, FOURIER_WIDTH)
    y = pl.pallas_call(
        _dense_dft_kernel,
        grid=(bsz,),
        in_specs=[blk, blk, _const_spec(c.shape), _const_spec(s.shape)],
        out_specs=blk,
        out_shape=jax.ShapeDtypeStruct(shp, BF16),
        compiler_params=pltpu.CompilerParams(dimension_semantics=("parallel",),
                                             vmem_limit_bytes=VMEM_LIMIT),
        name="seq_dft_dense",
    )(a.reshape(shp), b.reshape(shp), c, s)
    return y.reshape(bsz * seq, FOURIER_WIDTH)


SEQ_DFT_N2 = 128
DENSE_DFT_MAX_SEQ = 2048


def _dft_stage1_kernel(a_ref, b_ref, m_ref, tc_ref, ts_ref, vr_ref, vi_ref):
    n1 = a_ref.shape[1]
    u = _dot(m_ref[...], jnp.concatenate([a_ref[0], b_ref[0]], axis=0))
    ur, ui = u[:n1], u[n1:]
    tc, ts = tc_ref[...], ts_ref[...]
    vr_ref[0] = (ur * tc - ui * ts).astype(BF16)
    vi_ref[0] = (ur * ts + ui * tc).astype(BF16)


def _dft_stage2_kernel(vr_ref, vi_ref, m_ref, y_ref):
    for i in range(vr_ref.shape[1]):
        v = jnp.concatenate([vr_ref[0, i], vi_ref[0, i]], axis=0)
        y_ref[0, :, i * FOURIER_WIDTH:(i + 1) * FOURIER_WIDTH] = _dot(m_ref[...], v).astype(BF16)


def _seq_dft(a, b, bsz, seq, tcols, group):
    n2 = SEQ_DFT_N2
    n1 = seq // n2
    cols = n2 * FOURIER_WIDTH
    j1 = np.arange(n1)
    ang1 = 2.0 * np.pi * ((j1[:, None] * j1[None, :]) % n1) / n1
    c1, s1 = np.cos(ang1), np.sin(ang1)
    m1 = jnp.asarray(np.block([[c1, -s1], [s1, c1]]) / math.sqrt(seq), dtype=BF16)
    j2 = np.arange(n2)
    ang2 = 2.0 * np.pi * ((j2[:, None] * j2[None, :]) % n2) / n2
    m3 = jnp.asarray(np.concatenate([np.cos(ang2), -np.sin(ang2)], axis=1), dtype=BF16)
    angt = (jnp.arange(n1, dtype=F32)[:, None] * jnp.arange(n2, dtype=F32)[None, :]) * (2.0 * math.pi / seq)
    tcos = jnp.repeat(jnp.cos(angt), FOURIER_WIDTH, axis=1)
    tsin = jnp.repeat(jnp.sin(angt), FOURIER_WIDTH, axis=1)

    blk = pl.BlockSpec((1, n1, tcols), lambda c, bi: (bi, 0, c))
    twd = pl.BlockSpec((n1, tcols), lambda c, bi: (0, c))
    vr, vi = pl.pallas_call(
        _dft_stage1_kernel,
        grid=(cols // tcols, bsz),
        in_specs=[blk, blk, _const_spec(m1.shape), twd, twd],
        out_specs=[blk, blk],
        out_shape=[jax.ShapeDtypeStruct((bsz, n1, cols), BF16)] * 2,
        compiler_params=pltpu.CompilerParams(dimension_semantics=("parallel", "parallel"),
                                             vmem_limit_bytes=VMEM_LIMIT),
        name="seq_dft_stage1",
    )(a.reshape(bsz, n1, cols), b.reshape(bsz, n1, cols), m1, tcos, tsin)

    vblk = pl.BlockSpec((1, group, n2, FOURIER_WIDTH), lambda bi, g: (bi, g, 0, 0))
    y = pl.pallas_call(
        _dft_stage2_kernel,
        grid=(bsz, n1 // group),
        in_specs=[vblk, vblk, _const_spec(m3.shape)],
        out_specs=pl.BlockSpec((1, n2, group * FOURIER_WIDTH), lambda bi, g: (bi, 0, g)),
        out_shape=jax.ShapeDtypeStruct((bsz, n2, n1 * FOURIER_WIDTH), BF16),
        compiler_params=pltpu.CompilerParams(dimension_semantics=("parallel", "parallel"),
                                             vmem_limit_bytes=VMEM_LIMIT),
        name="seq_dft_stage2",
    )(vr.reshape(bsz, n1, n2, FOURIER_WIDTH), vi.reshape(bsz, n1, n2, FOURIER_WIDTH), m3)
    return y.reshape(bsz * seq, FOURIER_WIDTH)


def _lambda_kernel(lam_init, q1_ref, k1_ref, q2_ref, k2_ref, o_ref):
    s1 = jnp.sum(q1_ref[...] * k1_ref[...], axis=-1, keepdims=True)
    s2 = jnp.sum(q2_ref[...] * k2_ref[...], axis=-1, keepdims=True)
    o_ref[...] = jnp.exp(s1) - jnp.exp(s2) + lam_init


def _lambda(lam_init, q1, k1, q2, k2):
    vec = lambda a: a.reshape(1, QK_HEAD_DIM).astype(F32)
    return pl.pallas_call(
        functools.partial(_lambda_kernel, lam_init),
        out_shape=jax.ShapeDtypeStruct((1, 1), F32),
        name="diff_lambda",
    )(vec(q1), vec(k1), vec(q2), vec(k2))


ATTN_RESIDENT_MAX_SEQ = 2048


def _attn_kernel(tq, tk, unroll, fin_group, resident, out_scale, lam_ref, q1_ref, q2_ref, k_ref, v_ref, sub_ref, o_ref,
                 s0_ref, s1_ref, p0_ref, p1_ref, a0_ref, a1_ref, m0_ref, m1_ref, acc_ref):
    s_refs, p_refs, a_refs, m_refs = (s0_ref, s1_ref), (p0_ref, p1_ref), (a0_ref, a1_ref), (m0_ref, m1_ref)
    seq = k_ref.shape[2]
    head = pl.program_id(1)
    nq, nk = seq // tq, seq // tk
    n_steps = nq * nk
    assert nk & (nk - 1) == 0 and nq % fin_group == 0
    assert unroll % 2 == 0 and n_steps % unroll == 0
    nk_shift = nk.bit_length() - 1
    lam = lam_ref[0, 0]
    nt = (((1,), (1,)), ((), ()))
    ones = jnp.ones((tk, V_HEAD_DIM), BF16)

    def split(t):
        return lax.shift_right_logical(t, nk_shift), lax.bitwise_and(t, nk - 1)

    def scores(t, slot, hd):
        qi, kj = split(t)
        hd = hd if resident else 0
        rows = pl.ds(pl.multiple_of(qi * tq, tq), tq)
        qz = jnp.concatenate([q1_ref[0, hd, rows, :], q2_ref[0, hd, rows, :]], axis=0)
        kt = k_ref[0, hd, pl.ds(pl.multiple_of(kj * tk, tk), tk), :]
        s_refs[slot][...] = lax.dot_general(qz, kt, nt, preferred_element_type=F32)

    def softmax(t, slot, can_be_first):
        _, kj = split(t)
        s_ref, m_in, m_out = s_refs[slot], m_refs[slot], m_refs[1 - slot]
        for g in range(2 * tq // 8):
            rows = slice(g * 8, (g + 1) * 8)
            blocks = [s_ref[rows, j * LANES:(j + 1) * LANES] for j in range(tk // LANES)]
            m_old = m_in[rows, :]
            if can_be_first:
                m_old = jnp.where(kj == 0, -jnp.inf, m_old)
            blk_max = functools.reduce(jnp.maximum, blocks)
            m_new = jnp.maximum(m_old, jnp.max(blk_max, axis=-1, keepdims=True))
            m_out[rows, :] = m_new
            a_refs[slot][rows, :] = jnp.exp2(m_old - m_new)
            p_refs[slot][rows, :] = jnp.concatenate(
                [jnp.exp2(b - m_new) for b in blocks], axis=1).astype(BF16)

    def pv(u, slot, tile, can_be_first):
        _, kj = split(u)
        vt = v_ref[0, head if resident else 0, pl.ds(pl.multiple_of(kj * tk, tk), tk), :]
        v1 = jnp.concatenate([vt, ones], axis=1)
        alpha = a_refs[slot][...]
        alpha2 = jnp.concatenate([alpha, alpha], axis=1)
        acc = acc_ref[tile]
        if can_be_first:
            acc = jnp.where(kj == 0, 0.0, acc)
        acc_ref[tile] = alpha2 * acc + _dot(p_refs[slot][...], v1)

    @pl.when((pl.program_id(0) == 0) & (pl.program_id(1) == 0))
    def _():
        acc_ref[...] = jnp.zeros(acc_ref.shape, F32)
        m0_ref[...] = jnp.zeros(m0_ref.shape, F32)

    if resident:
        pl.when(head == 0)(lambda: scores(0, 0, 0))
    else:
        scores(0, 0, head)

    def body(i, _):
        t = unroll * i
        for j in range(unroll):
            nxt = t + j + 1
            if j < unroll - 1:
                scores(nxt, (j + 1) % 2, head)
            elif resident:
                wrap = nxt >= n_steps
                scores(jnp.where(wrap, 0, nxt), 0, jnp.where(wrap, jnp.minimum(head + 1, N_HEADS - 1), head))
            else:
                scores(jnp.minimum(nxt, n_steps - 1), 0, head)
            first = j % math.gcd(unroll, nk) == 0
            softmax(t + j, j % 2, first)
            pv(t + j, j % 2, split(t + j)[0], first)
        return 0

    lax.fori_loop(0, n_steps // unroll, body, 0)

    sub = sub_ref[...] * out_scale

    def finish(i, _):
        for j in range(fin_group):
            tile = i * fin_group + j
            a = acc_ref[tile]
            o = a[:, :V_HEAD_DIM] / a[:, V_HEAD_DIM:]
            att = o[:tq] - lam * o[tq:]
            o_ref[0, pl.ds(pl.multiple_of(tile * tq, tq), tq), :] = _rms(att, sub).astype(BF16)
        return 0

    lax.fori_loop(0, nq // fin_group, finish, 0)


def _diff_attn(lam, q1, q2, k, v, subln, out_scale, tq, tk, unroll, fin_group):
    b, _, seq, _ = k.shape
    resident = seq <= ATTN_RESIDENT_MAX_SEQ
    if resident:
        head = pl.BlockSpec((1, N_HEADS, seq, V_HEAD_DIM), lambda bi, h: (bi, 0, 0, 0))
    else:
        head = pl.BlockSpec((1, 1, seq, V_HEAD_DIM), lambda bi, h: (bi, h, 0, 0))
    return pl.pallas_call(
        functools.partial(_attn_kernel, tq, tk, unroll, fin_group, resident, out_scale),
        grid=(b, N_HEADS),
        in_specs=[pl.BlockSpec(memory_space=pltpu.SMEM), head, head, head, head,
                  _const_spec((1, V_HEAD_DIM))],
        out_specs=pl.BlockSpec((1, seq, V_HEAD_DIM), lambda bi, h: (bi, 0, h)),
        out_shape=jax.ShapeDtypeStruct((b, seq, ATTN_WIDTH), BF16),
        scratch_shapes=[pltpu.VMEM((2 * tq, tk), F32)] * 2 + [pltpu.VMEM((2 * tq, tk), BF16)] * 2
        + [pltpu.VMEM((2 * tq, LANES), F32)] * 4
        + [pltpu.VMEM((seq // tq, 2 * tq, 2 * V_HEAD_DIM), F32)],
        compiler_params=pltpu.CompilerParams(dimension_semantics=("arbitrary", "arbitrary"),
                                             vmem_limit_bytes=VMEM_LIMIT),
        name="diff_attn",
    )(lam, q1, q2, k, v, subln)


def _merge_kernel(x_ref, y_ref, att_ref, gf_ref, ga_ref, wf_ref, wa_ref, wo_ref, g_ref, o_ref):
    fo = _dot(y_ref[...], wf_ref[...])
    ao = _dot(att_ref[...], wa_ref[...])
    merged = gf_ref[...].astype(F32) * fo + ga_ref[...].astype(F32) * ao
    o_ref[...] = x_ref[...] + _rms(_dot(merged.astype(BF16), wo_ref[...]), g_ref[...])


def _merge(x2d, y2, att2d, gf, ga, w_fourier, w_attn, w_out, gain, tm):
    n_tok = x2d.shape[0]
    tok = lambda w: pl.BlockSpec((tm, w), lambda i: (i, 0))
    return pl.pallas_call(
        _merge_kernel,
        grid=(n_tok // tm,),
        in_specs=[tok(D_MODEL), tok(FOURIER_WIDTH), tok(ATTN_WIDTH), tok(D_MODEL), tok(D_MODEL),
                  _const_spec(w_fourier.shape), _const_spec(w_attn.shape),
                  _const_spec(w_out.shape), _const_spec((1, D_MODEL))],
        out_specs=tok(D_MODEL),
        out_shape=jax.ShapeDtypeStruct((n_tok, D_MODEL), F32),
        compiler_params=pltpu.CompilerParams(dimension_semantics=("parallel",),
                                             vmem_limit_bytes=VMEM_LIMIT),
        name="merge",
    )(x2d, y2, att2d, gf, ga, w_fourier, w_attn, w_out, gain)


def _mlp_ple_kernel(x_ref, p_ref, gpre_ref, wup_ref, wdn_ref, gpost_ref, wple_ref, wgate_ref,
                    gple_ref, o_ref):
    x = x_ref[...]
    h = _rms(x, gpre_ref[...]).astype(BF16)
    d = jnp.zeros(x.shape, F32)
    for c in range(D_FF // D_MODEL):
        cols = slice(c * D_MODEL, (c + 1) * D_MODEL)
        u = jnp.square(jnp.maximum(_dot(h, wup_ref[:, cols]), 0.0))
        d = d + _dot(u.astype(BF16), wdn_ref[cols, :])
    x = x + _rms(d, gpost_ref[...])
    e = _dot(p_ref[...].astype(BF16), wple_ref[...]) * _sigmoid(_dot(x.astype(BF16), wgate_ref[...]))
    o_ref[...] = x + _rms(e, gple_ref[...])


def _mlp_ple(x2d, p2d, g_pre, w_up, w_down, g_post, w_ple, w_gate, g_ple, tm):
    n_tok = x2d.shape[0]
    tok = lambda w: pl.BlockSpec((tm, w), lambda i: (i, 0))
    vec = _const_spec((1, D_MODEL))
    return pl.pallas_call(
        _mlp_ple_kernel,
        grid=(n_tok // tm,),
        in_specs=[tok(D_MODEL), tok(PLE_DIM), vec, _const_spec(w_up.shape),
                  _const_spec(w_down.shape), vec, _const_spec(w_ple.shape),
                  _const_spec(w_gate.shape), vec],
        out_specs=tok(D_MODEL),
        out_shape=jax.ShapeDtypeStruct((n_tok, D_MODEL), F32),
        compiler_params=pltpu.CompilerParams(dimension_semantics=("parallel",),
                                             vmem_limit_bytes=VMEM_LIMIT),
        name="mlp_ple",
    )(x2d, p2d, g_pre, w_up, w_down, g_post, w_ple, w_gate, g_ple)


def _layer(x, p, layer_idx, prm):
    (norm_mix_pre, w_in, w_fourier, w_attn, w_out, lq1, lk1, lq2, lk2, subln,
     norm_mix_post, norm_mlp_pre, w_up, w_down, norm_mlp_post, w_ple, w_gate, norm_ple_post) = prm
    b, seq, _ = x.shape
    n_tok = b * seq
    row = lambda g: g.reshape(1, -1).astype(F32)
    x2d = x.reshape(n_tok, D_MODEL)

    fa, fb, q1, q2, k, v, gf, ga = _in_proj(x2d, row(norm_mix_pre), w_in, seq, tm=512)
    if seq <= DENSE_DFT_MAX_SEQ:
        y2 = _seq_dft_dense(fa, fb, b, seq)
    else:
        y2 = _seq_dft(fa, fb, b, seq, tcols=(1 << 19) * SEQ_DFT_N2 // seq, group=8)

    lam_init = 0.8 - 0.6 * math.exp(-0.3 * layer_idx)
    lam = _lambda(lam_init, lq1, lk1, lq2, lk2)
    att = _diff_attn(lam, q1, q2, k, v, row(subln),
                     1.0 - lam_init, tq=256, tk=1024, unroll=8, fin_group=8)

    x1 = _merge(x2d, y2, att.reshape(n_tok, ATTN_WIDTH), gf, ga, w_fourier, w_attn, w_out,
                row(norm_mix_post), tm=1024)
    out = _mlp_ple(x1, p.reshape(n_tok, PLE_DIM), row(norm_mlp_pre), w_up, w_down,
                   row(norm_mlp_post), w_ple, w_gate, row(norm_ple_post), tm=1024)
    return out.reshape(b, seq, D_MODEL)


def _prepare_weights(w_in, w_fourier, w_attn, w_out, w_up, w_down, w_ple, w_gate):
    w_in = w_in.astype(BF16)
    w_in = jnp.concatenate([_fold_chan_dft(w_in[:, :FOURIER_WIDTH]), w_in[:, FOURIER_WIDTH:]], axis=1)
    return (w_in,) + tuple(w.astype(BF16) for w in (w_fourier, w_attn, w_out, w_up, w_down, w_ple, w_gate))


def kernel(x_prompt, x_sample, p_prompt, p_sample, norm_mix_pre, w_in, w_fourier, w_attn, w_out,
           lambda_q1, lambda_k1, lambda_q2, lambda_k2, subln, norm_mix_post, norm_mlp_pre,
           w_up, w_down, norm_mlp_post, w_ple, w_ple_gate, norm_ple_post):
    y_prompt, y_sample = x_prompt, x_sample
    for i in range(w_in.shape[0]):
        wi, wf, wa, wo, wu, wd, wp, wg = _prepare_weights(
            w_in[i], w_fourier[i], w_attn[i], w_out[i], w_up[i], w_down[i], w_ple[i], w_ple_gate[i])
        prm = (norm_mix_pre[i], wi, wf, wa, wo,
               lambda_q1[i], lambda_k1[i], lambda_q2[i], lambda_k2[i], subln[i],
               norm_mix_post[i], norm_mlp_pre[i], wu, wd, norm_mlp_post[i], wp, wg, norm_ple_post[i])
        y_prompt = _layer(y_prompt, p_prompt[i], i, prm)
        y_sample = _layer(y_sample, p_sample[i], i, prm)
    return (y_prompt, y_sample)
```

```python
import functools
import math

import jax
import jax.numpy as jnp
import numpy as np
from jax import lax
from jax.experimental import pallas as pl
from jax.experimental.pallas import tpu as pltpu

D_MODEL = 1024
N_HEADS = 8
QK_HEAD_DIM = 64
V_HEAD_DIM = 128
Q_WIDTH = N_HEADS * 2 * QK_HEAD_DIM
ATTN_WIDTH = N_HEADS * V_HEAD_DIM
N_FOURIER_GROUPS = 4
FOURIER_GROUP_DIM = 128
FOURIER_WIDTH = N_FOURIER_GROUPS * FOURIER_GROUP_DIM
ROPE_DIM = QK_HEAD_DIM // 4
ROPE_THETA = 500000.0
D_FF = 4 * D_MODEL
PLE_DIM = 256
EPS = 1e-6

LANES = 128
VMEM_LIMIT = 56 * 1024 * 1024
F32 = jnp.float32
BF16 = jnp.bfloat16
LOG2E = 1.4426950408889634
ROW_CHUNK = 256


def _rms(x, g):
    return x * lax.rsqrt(jnp.mean(x * x, axis=-1, keepdims=True) + EPS) * g


def _dot(a, b):
    return jnp.dot(a, b, preferred_element_type=F32)


def _sigmoid(x):
    return 1.0 / (1.0 + jnp.exp(-x))


def _const_spec(shape):
    nd = len(shape)
    return pl.BlockSpec(shape, lambda *_: (0,) * nd, pipeline_mode=pl.Buffered(1))


def _rope(y, c, sa, sb):
    outs = []
    for j in range(y.shape[1] // LANES):
        yc = y[:, j * LANES:(j + 1) * LANES]
        outs.append(yc * c + pltpu.roll(yc, ROPE_DIM // 2, 1) * sa
                    + pltpu.roll(yc, LANES - ROPE_DIM // 2, 1) * sb)
    return jnp.concatenate(outs, axis=1)


def _store_heads(ref, rows, val):
    for hd in range(N_HEADS):
        ref[0, hd, rows, :] = val[:, hd * V_HEAD_DIM:(hd + 1) * V_HEAD_DIM]


def _in_proj_kernel(x_ref, g_ref, w_ref, cq_ref, sqa_ref, sqb_ref, ck_ref, ska_ref, skb_ref,
                    fa_ref, fb_ref, q1_ref, q2_ref, k_ref, v_ref, gf_ref, ga_ref):
    comp1 = lax.broadcasted_iota(jnp.int32, (1, Q_WIDTH), 1) % V_HEAD_DIM < QK_HEAD_DIM
    for r in range(x_ref.shape[0] // ROW_CHUNK):
        rows = slice(r * ROW_CHUNK, (r + 1) * ROW_CHUNK)
        h = _rms(x_ref[rows, :], g_ref[...]).astype(BF16)
        o = 0
        fa_ref[rows, :] = _dot(h, w_ref[:, o:o + FOURIER_WIDTH]).astype(BF16); o += FOURIER_WIDTH
        fb_ref[rows, :] = _dot(h, w_ref[:, o:o + FOURIER_WIDTH]).astype(BF16); o += FOURIER_WIDTH
        q = _dot(h, w_ref[:, o:o + Q_WIDTH]); o += Q_WIDTH
        q = _rope(q, cq_ref[rows, :], sqa_ref[rows, :], sqb_ref[rows, :])
        _store_heads(q1_ref, rows, jnp.where(comp1, q, 0.0).astype(BF16))
        _store_heads(q2_ref, rows, jnp.where(comp1, 0.0, q).astype(BF16))
        k = _dot(h, w_ref[:, o:o + Q_WIDTH]); o += Q_WIDTH
        _store_heads(k_ref, rows, _rope(k, ck_ref[rows, :], ska_ref[rows, :], skb_ref[rows, :]).astype(BF16))
        _store_heads(v_ref, rows, _dot(h, w_ref[:, o:o + ATTN_WIDTH]).astype(BF16)); o += ATTN_WIDTH
        gf_ref[rows, :] = _sigmoid(_dot(h, w_ref[:, o:o + D_MODEL])).astype(BF16); o += D_MODEL
        ga_ref[rows, :] = _sigmoid(_dot(h, w_ref[:, o:o + D_MODEL])).astype(BF16)


def _rope_tables(seq, scale):
    half = ROPE_DIM // 2
    pos = jnp.arange(seq, dtype=F32)
    inv_freq = ROPE_THETA ** (-(jnp.arange(0, ROPE_DIM, 2, dtype=F32) / ROPE_DIM))
    ang = pos[:, None] * inv_freq[None, :]
    cos, sin = jnp.cos(ang), jnp.sin(ang)
    ones = jnp.ones((seq, QK_HEAD_DIM - ROPE_DIM), F32)
    zeros = jnp.zeros((seq, QK_HEAD_DIM - ROPE_DIM), F32)
    zh = jnp.zeros((seq, half), F32)
    c = jnp.concatenate([cos, cos, ones], axis=1)
    sa = jnp.concatenate([zh, sin, zeros], axis=1)
    sb = jnp.concatenate([-sin, zh, zeros], axis=1)
    rep = LANES // QK_HEAD_DIM
    return tuple(jnp.tile(t * scale, (1, rep)) for t in (c, sa, sb))


def _in_proj(x2d, gain, w_in, seq, tm):
    n_tok = x2d.shape[0]
    q_scale = LOG2E / math.sqrt(QK_HEAD_DIM)
    tabs = _rope_tables(seq, q_scale) + _rope_tables(seq, 1.0)
    spt = seq // tm
    tok = lambda w: pl.BlockSpec((tm, w), lambda i: (i, 0))
    tab = pl.BlockSpec((tm, LANES), lambda i: (i % spt, 0))
    heads = pl.BlockSpec((1, N_HEADS, tm, V_HEAD_DIM), lambda i: (i // spt, 0, i % spt, 0))
    heads_shape = jax.ShapeDtypeStruct((n_tok // seq, N_HEADS, seq, V_HEAD_DIM), BF16)
    flat = lambda w: (tok(w), jax.ShapeDtypeStruct((n_tok, w), BF16))
    outs = [flat(FOURIER_WIDTH)] * 2 + [(heads, heads_shape)] * 4 + [flat(D_MODEL)] * 2
    return pl.pallas_call(
        _in_proj_kernel,
        grid=(n_tok // tm,),
        in_specs=[tok(D_MODEL), _const_spec((1, D_MODEL)), _const_spec(w_in.shape)] + [tab] * 6,
        out_specs=[spec for spec, _ in outs],
        out_shape=[shape for _, shape in outs],
        compiler_params=pltpu.CompilerParams(dimension_semantics=("parallel",),
                                             vmem_limit_bytes=VMEM_LIMIT),
        name="in_proj",
    )(x2d, gain, w_in, *tabs)


def _chan_dft_weights():
    n = FOURIER_GROUP_DIM
    idx = np.arange(n)
    ang = 2.0 * np.pi * ((idx[:, None] * idx[None, :]) % n) / n
    eye = np.eye(N_FOURIER_GROUPS)
    c = np.kron(eye, np.cos(ang)) / math.sqrt(n)
    s = np.kron(eye, np.sin(ang)) / math.sqrt(n)
    return jnp.asarray(np.concatenate([c, s], axis=1), dtype=BF16)


def _fold_kernel(a_ref, b_ref, o_ref):
    o_ref[...] = _dot(a_ref[...], b_ref[...]).astype(BF16)


def _fold_chan_dft(w_f):
    return pl.pallas_call(
        _fold_kernel,
        out_shape=jax.ShapeDtypeStruct((w_f.shape[0], 2 * FOURIER_WIDTH), BF16),
        compiler_params=pltpu.CompilerParams(vmem_limit_bytes=VMEM_LIMIT),
        name="fold_chan_dft",
    )(w_f, _chan_dft_weights())


def _dense_dft_kernel(a_ref, b_ref, c_ref, s_ref, y_ref):
    y_ref[0] = (_dot(c_ref[...], a_ref[0]) + _dot(s_ref[...], b_ref[0])).astype(BF16)


def _seq_dft_dense(a, b, bsz, seq):
    lo_n = 64
    k = jnp.arange(seq, dtype=jnp.int32)[None, :]

    def cos_sin(rows):
        ang = ((rows[:, None] * k) % seq).astype(F32) * (2.0 * math.pi / seq)
        return jnp.cos(ang), jnp.sin(ang)

    (ch, sh), (cl, sl) = cos_sin(jnp.arange(seq // lo_n, dtype=jnp.int32) * lo_n), cos_sin(
        jnp.arange(lo_n, dtype=jnp.int32))
    sc = 1.0 / math.sqrt(seq)
    c = ((ch[:, None] * cl[None] - sh[:, None] * sl[None]) * sc).astype(BF16).reshape(seq, seq)
    s = ((sh[:, None] * cl[None] + ch[:, None] * sl[None]) * (-sc)).astype(BF16).reshape(seq, seq)
    blk = pl.BlockSpec((1, seq, FOURIER_WIDTH), lambda bi: (bi, 0, 0))
    shp = (bsz, seq, FOURIER_WIDTH)
    y = pl.pallas_call(
        _dense_dft_kernel,
        grid=(bsz,),
        in_specs=[blk, blk, _const_spec(c.shape), _const_spec(s.shape)],
        out_specs=blk,
        out_shape=jax.ShapeDtypeStruct(shp, BF16),
        compiler_params=pltpu.CompilerParams(dimension_semantics=("parallel",),
                                             vmem_limit_bytes=VMEM_LIMIT),
        name="seq_dft_dense",
    )(a.reshape(shp), b.reshape(shp), c, s)
    return y.reshape(bsz * seq, FOURIER_WIDTH)


SEQ_DFT_N2 = 128
DENSE_DFT_MAX_SEQ = 2048


def _dft_stage1_kernel(a_ref, b_ref, m_ref, tc_ref, ts_ref, vr_ref, vi_ref):
    n1 = a_ref.shape[1]
    u = _dot(m_ref[...], jnp.concatenate([a_ref[0], b_ref[0]], axis=0))
    ur, ui = u[:n1], u[n1:]
    tc, ts = tc_ref[...], ts_ref[...]
    vr_ref[0] = (ur * tc - ui * ts).astype(BF16)
    vi_ref[0] = (ur * ts + ui * tc).astype(BF16)


def _dft_stage2_kernel(vr_ref, vi_ref, m_ref, y_ref):
    for i in range(vr_ref.shape[1]):
        v = jnp.concatenate([vr_ref[0, i], vi_ref[0, i]], axis=0)
        y_ref[0, :, i * FOURIER_WIDTH:(i + 1) * FOURIER_WIDTH] = _dot(m_ref[...], v).astype(BF16)


def _seq_dft(a, b, bsz, seq, tcols, group):
    n2 = SEQ_DFT_N2
    n1 = seq // n2
    cols = n2 * FOURIER_WIDTH
    j1 = np.arange(n1)
    ang1 = 2.0 * np.pi * ((j1[:, None] * j1[None, :]) % n1) / n1
    c1, s1 = np.cos(ang1), np.sin(ang1)
    m1 = jnp.asarray(np.block([[c1, -s1], [s1, c1]]) / math.sqrt(seq), dtype=BF16)
    j2 = np.arange(n2)
    ang2 = 2.0 * np.pi * ((j2[:, None] * j2[None, :]) % n2) / n2
    m3 = jnp.asarray(np.concatenate([np.cos(ang2), -np.sin(ang2)], axis=1), dtype=BF16)
    angt = (jnp.arange(n1, dtype=F32)[:, None] * jnp.arange(n2, dtype=F32)[None, :]) * (2.0 * math.pi / seq)
    tcos = jnp.repeat(jnp.cos(angt), FOURIER_WIDTH, axis=1)
    tsin = jnp.repeat(jnp.sin(angt), FOURIER_WIDTH, axis=1)

    blk = pl.BlockSpec((1, n1, tcols), lambda c, bi: (bi, 0, c))
    twd = pl.BlockSpec((n1, tcols), lambda c, bi: (0, c))
    vr, vi = pl.pallas_call(
        _dft_stage1_kernel,
        grid=(cols // tcols, bsz),
        in_specs=[blk, blk, _const_spec(m1.shape), twd, twd],
        out_specs=[blk, blk],
        out_shape=[jax.ShapeDtypeStruct((bsz, n1, cols), BF16)] * 2,
        compiler_params=pltpu.CompilerParams(dimension_semantics=("parallel", "parallel"),
                                             vmem_limit_bytes=VMEM_LIMIT),
        name="seq_dft_stage1",
    )(a.reshape(bsz, n1, cols), b.reshape(bsz, n1, cols), m1, tcos, tsin)

    vblk = pl.BlockSpec((1, group, n2, FOURIER_WIDTH), lambda bi, g: (bi, g, 0, 0))
    y = pl.pallas_call(
        _dft_stage2_kernel,
        grid=(bsz, n1 // group),
        in_specs=[vblk, vblk, _const_spec(m3.shape)],
        out_specs=pl.BlockSpec((1, n2, group * FOURIER_WIDTH), lambda bi, g: (bi, 0, g)),
        out_shape=jax.ShapeDtypeStruct((bsz, n2, n1 * FOURIER_WIDTH), BF16),
        compiler_params=pltpu.CompilerParams(dimension_semantics=("parallel", "parallel"),
                                             vmem_limit_bytes=VMEM_LIMIT),
        name="seq_dft_stage2",
    )(vr.reshape(bsz, n1, n2, FOURIER_WIDTH), vi.reshape(bsz, n1, n2, FOURIER_WIDTH), m3)
    return y.reshape(bsz * seq, FOURIER_WIDTH)


def _lambda_kernel(lam_init, q1_ref, k1_ref, q2_ref, k2_ref, o_ref):
    s1 = jnp.sum(q1_ref[...] * k1_ref[...], axis=-1, keepdims=True)
    s2 = jnp.sum(q2_ref[...] * k2_ref[...], axis=-1, keepdims=True)
    o_ref[...] = jnp.exp(s1) - jnp.exp(s2) + lam_init


def _lambda(lam_init, q1, k1, q2, k2):
    vec = lambda a: a.reshape(1, QK_HEAD_DIM).astype(F32)
    return pl.pallas_call(
        functools.partial(_lambda_kernel, lam_init),
        out_shape=jax.ShapeDtypeStruct((1, 1), F32),
        name="diff_lambda",
    )(vec(q1), vec(k1), vec(q2), vec(k2))


ATTN_RESIDENT_MAX_SEQ = 2048


def _attn_kernel(tq, tk, unroll, fin_group, resident, out_scale, lam_ref, q1_ref, q2_ref, k_ref, v_ref, sub_ref, o_ref,
                 s0_ref, s1_ref, p0_ref, p1_ref, a0_ref, a1_ref, m0_ref, m1_ref, acc_ref):
    s_refs, p_refs, a_refs, m_refs = (s0_ref, s1_ref), (p0_ref, p1_ref), (a0_ref, a1_ref), (m0_ref, m1_ref)
    seq = k_ref.shape[2]
    head = pl.program_id(1)
    nq, nk = seq // tq, seq // tk
    n_steps = nq * nk
    assert nk & (nk - 1) == 0 and nq % fin_group == 0
    assert unroll % 2 == 0 and n_steps % unroll == 0
    nk_shift = nk.bit_length() - 1
    lam = lam_ref[0, 0]
    nt = (((1,), (1,)), ((), ()))
    ones = jnp.ones((tk, V_HEAD_DIM), BF16)

    def split(t):
        return lax.shift_right_logical(t, nk_shift), lax.bitwise_and(t, nk - 1)

    def scores(t, slot, hd):
        qi, kj = split(t)
        hd = hd if resident else 0
        rows = pl.ds(pl.multiple_of(qi * tq, tq), tq)
        qz = jnp.concatenate([q1_ref[0, hd, rows, :], q2_ref[0, hd, rows, :]], axis=0)
        kt = k_ref[0, hd, pl.ds(pl.multiple_of(kj * tk, tk), tk), :]
        s_refs[slot][...] = lax.dot_general(qz, kt, nt, preferred_element_type=F32)

    def softmax(t, slot, can_be_first):
        _, kj = split(t)
        s_ref, m_in, m_out = s_refs[slot], m_refs[slot], m_refs[1 - slot]
        for g in range(2 * tq // 8):
            rows = slice(g * 8, (g + 1) * 8)
            blocks = [s_ref[rows, j * LANES:(j + 1) * LANES] for j in range(tk // LANES)]
            m_old = m_in[rows, :]
            if can_be_first:
                m_old = jnp.where(kj == 0, -jnp.inf, m_old)
            blk_max = functools.reduce(jnp.maximum, blocks)
            m_new = jnp.maximum(m_old, jnp.max(blk_max, axis=-1, keepdims=True))
            m_out[rows, :] = m_new
            a_refs[slot][rows, :] = jnp.exp2(m_old - m_new)
            p_refs[slot][rows, :] = jnp.concatenate(
                [jnp.exp2(b - m_new) for b in blocks], axis=1).astype(BF16)

    def pv(u, slot, tile, can_be_first):
        _, kj = split(u)
        vt = v_ref[0, head if resident else 0, pl.ds(pl.multiple_of(kj * tk, tk), tk), :]
        v1 = jnp.concatenate([vt, ones], axis=1)
        alpha = a_refs[slot][...]
        alpha2 = jnp.concatenate([alpha, alpha], axis=1)
        acc = acc_ref[tile]
        if can_be_first:
            acc = jnp.where(kj == 0, 0.0, acc)
        acc_ref[tile] = alpha2 * acc + _dot(p_refs[slot][...], v1)

    @pl.when((pl.program_id(0) == 0) & (pl.program_id(1) == 0))
    def _():
        acc_ref[...] = jnp.zeros(acc_ref.shape, F32)
        m0_ref[...] = jnp.zeros(m0_ref.shape, F32)

    if resident:
        pl.when(head == 0)(lambda: scores(0, 0, 0))
    else:
        scores(0, 0, head)

    def body(i, _):
        t = unroll * i
        for j in range(unroll):
            nxt = t + j + 1
            if j < unroll - 1:
                scores(nxt, (j + 1) % 2, head)
            elif resident:
                wrap = nxt >= n_steps
                scores(jnp.where(wrap, 0, nxt), 0, jnp.where(wrap, jnp.minimum(head + 1, N_HEADS - 1), head))
            else:
                scores(jnp.minimum(nxt, n_steps - 1), 0, head)
            first = j % math.gcd(unroll, nk) == 0
            softmax(t + j, j % 2, first)
            pv(t + j, j % 2, split(t + j)[0], first)
        return 0

    lax.fori_loop(0, n_steps // unroll, body, 0)

    sub = sub_ref[...] * out_scale

    def finish(i, _):
        for j in range(fin_group):
            tile = i * fin_group + j
            a = acc_ref[tile]
            o = a[:, :V_HEAD_DIM] / a[:, V_HEAD_DIM:]
            att = o[:tq] - lam * o[tq:]
            o_ref[0, pl.ds(pl.multiple_of(tile * tq, tq), tq), :] = _rms(att, sub).astype(BF16)
        return 0

    lax.fori_loop(0, nq // fin_group, finish, 0)


def _diff_attn(lam, q1, q2, k, v, subln, out_scale, tq, tk, unroll, fin_group):
    b, _, seq, _ = k.shape
    resident = seq <= ATTN_RESIDENT_MAX_SEQ
    if resident:
        head = pl.BlockSpec((1, N_HEADS, seq, V_HEAD_DIM), lambda bi, h: (bi, 0, 0, 0))
    else:
        head = pl.BlockSpec((1, 1, seq, V_HEAD_DIM), lambda bi, h: (bi, h, 0, 0))
    return pl.pallas_call(
        functools.partial(_attn_kernel, tq, tk, unroll, fin_group, resident, out_scale),
        grid=(b, N_HEADS),
        in_specs=[pl.BlockSpec(memory_space=pltpu.SMEM), head, head, head, head,
                  _const_spec((1, V_HEAD_DIM))],
        out_specs=pl.BlockSpec((1, seq, V_HEAD_DIM), lambda bi, h: (bi, 0, h)),
        out_shape=jax.ShapeDtypeStruct((b, seq, ATTN_WIDTH), BF16),
        scratch_shapes=[pltpu.VMEM((2 * tq, tk), F32)] * 2 + [pltpu.VMEM((2 * tq, tk), BF16)] * 2
        + [pltpu.VMEM((2 * tq, LANES), F32)] * 4
        + [pltpu.VMEM((seq // tq, 2 * tq, 2 * V_HEAD_DIM), F32)],
        compiler_params=pltpu.CompilerParams(dimension_semantics=("arbitrary", "arbitrary"),
                                             vmem_limit_bytes=VMEM_LIMIT),
        name="diff_attn",
    )(lam, q1, q2, k, v, subln)


def _merge_kernel(x_ref, y_ref, att_ref, gf_ref, ga_ref, wf_ref, wa_ref, wo_ref, g_ref, o_ref):
    for r in range(x_ref.shape[0] // ROW_CHUNK):
        rows = slice(r * ROW_CHUNK, (r + 1) * ROW_CHUNK)
        fo = _dot(y_ref[rows, :], wf_ref[...])
        ao = _dot(att_ref[rows, :], wa_ref[...])
        merged = gf_ref[rows, :].astype(F32) * fo + ga_ref[rows, :].astype(F32) * ao
        o_ref[rows, :] = x_ref[rows, :] + _rms(_dot(merged.astype(BF16), wo_ref[...]), g_ref[...])


def _merge(x2d, y2, att2d, gf, ga, w_fourier, w_attn, w_out, gain, tm):
    n_tok = x2d.shape[0]
    tok = lambda w: pl.BlockSpec((tm, w), lambda i: (i, 0))
    return pl.pallas_call(
        _merge_kernel,
        grid=(n_tok // tm,),
        in_specs=[tok(D_MODEL), tok(FOURIER_WIDTH), tok(ATTN_WIDTH), tok(D_MODEL), tok(D_MODEL),
                  _const_spec(w_fourier.shape), _const_spec(w_attn.shape),
                  _const_spec(w_out.shape), _const_spec((1, D_MODEL))],
        out_specs=tok(D_MODEL),
        out_shape=jax.ShapeDtypeStruct((n_tok, D_MODEL), F32),
        compiler_params=pltpu.CompilerParams(dimension_semantics=("parallel",),
                                             vmem_limit_bytes=VMEM_LIMIT),
        name="merge",
    )(x2d, y2, att2d, gf, ga, w_fourier, w_attn, w_out, gain)


def _mlp_ple_kernel(x_ref, p_ref, gpre_ref, wup_ref, wdn_ref, gpost_ref, wple_ref, wgate_ref,
                    gple_ref, o_ref):
    for r in range(x_ref.shape[0] // ROW_CHUNK):
        rows = slice(r * ROW_CHUNK, (r + 1) * ROW_CHUNK)
        x = x_ref[rows, :]
        h = _rms(x, gpre_ref[...]).astype(BF16)
        d = jnp.zeros(x.shape, F32)
        for c in range(D_FF // D_MODEL):
            cols = slice(c * D_MODEL, (c + 1) * D_MODEL)
            u = jnp.square(jnp.maximum(_dot(h, wup_ref[:, cols]), 0.0))
            d = d + _dot(u.astype(BF16), wdn_ref[cols, :])
        x = x + _rms(d, gpost_ref[...])
        e = _dot(p_ref[rows, :].astype(BF16), wple_ref[...]) * _sigmoid(_dot(x.astype(BF16), wgate_ref[...]))
        o_ref[rows, :] = x + _rms(e, gple_ref[...])


def _mlp_ple(x2d, p2d, g_pre, w_up, w_down, g_post, w_ple, w_gate, g_ple, tm):
    n_tok = x2d.shape[0]
    tok = lambda w: pl.BlockSpec((tm, w), lambda i: (i, 0))
    vec = _const_spec((1, D_MODEL))
    return pl.pallas_call(
        _mlp_ple_kernel,
        grid=(n_tok // tm,),
        in_specs=[tok(D_MODEL), tok(PLE_DIM), vec, _const_spec(w_up.shape),
                  _const_spec(w_down.shape), vec, _const_spec(w_ple.shape),
                  _const_spec(w_gate.shape), vec],
        out_specs=tok(D_MODEL),
        out_shape=jax.ShapeDtypeStruct((n_tok, D_MODEL), F32),
        compiler_params=pltpu.CompilerParams(dimension_semantics=("parallel",),
                                             vmem_limit_bytes=VMEM_LIMIT),
        name="mlp_ple",
    )(x2d, p2d, g_pre, w_up, w_down, g_post, w_ple, w_gate, g_ple)


def _layer(x, p, layer_idx, prm):
    (norm_mix_pre, w_in, w_fourier, w_attn, w_out, lq1, lk1, lq2, lk2, subln,
     norm_mix_post, norm_mlp_pre, w_up, w_down, norm_mlp_post, w_ple, w_gate, norm_ple_post) = prm
    b, seq, _ = x.shape
    n_tok = b * seq
    row = lambda g: g.reshape(1, -1).astype(F32)
    x2d = x.reshape(n_tok, D_MODEL)

    fa, fb, q1, q2, k, v, gf, ga = _in_proj(x2d, row(norm_mix_pre), w_in, seq, tm=512)
    if seq <= DENSE_DFT_MAX_SEQ:
        y2 = _seq_dft_dense(fa, fb, b, seq)
    else:
        y2 = _seq_dft(fa, fb, b, seq, tcols=(1 << 19) * SEQ_DFT_N2 // seq, group=8)

    lam_init = 0.8 - 0.6 * math.exp(-0.3 * layer_idx)
    lam = _lambda(lam_init, lq1, lk1, lq2, lk2)
    att = _diff_attn(lam, q1, q2, k, v, row(subln),
                     1.0 - lam_init, tq=256, tk=1024, unroll=8, fin_group=8)

    x1 = _merge(x2d, y2, att.reshape(n_tok, ATTN_WIDTH), gf, ga, w_fourier, w_attn, w_out,
                row(norm_mix_post), tm=1024)
    out = _mlp_ple(x1, p.reshape(n_tok, PLE_DIM), row(norm_mlp_pre), w_up, w_down,
                   row(norm_mlp_post), w_ple, w_gate, row(norm_ple_post), tm=1024)
    return out.reshape(b, seq, D_MODEL)


def _prepare_weights(w_in, w_fourier, w_attn, w_out, w_up, w_down, w_ple, w_gate):
    w_in = w_in.astype(BF16)
    w_in = jnp.concatenate([_fold_chan_dft(w_in[:, :FOURIER_WIDTH]), w_in[:, FOURIER_WIDTH:]], axis=1)
    return (w_in,) + tuple(w.astype(BF16) for w in (w_fourier, w_attn, w_out, w_up, w_down, w_ple, w_gate))


def kernel(x_prompt, x_sample, p_prompt, p_sample, norm_mix_pre, w_in, w_fourier, w_attn, w_out,
           lambda_q1, lambda_k1, lambda_q2, lambda_k2, subln, norm_mix_post, norm_mlp_pre,
           w_up, w_down, norm_mlp_post, w_ple, w_ple_gate, norm_ple_post):
    y_prompt, y_sample = x_prompt, x_sample
    for i in range(w_in.shape[0]):
        wi, wf, wa, wo, wu, wd, wp, wg = _prepare_weights(
            w_in[i], w_fourier[i], w_attn[i], w_out[i], w_up[i], w_down[i], w_ple[i], w_ple_gate[i])
        prm = (norm_mix_pre[i], wi, wf, wa, wo,
               lambda_q1[i], lambda_k1[i], lambda_q2[i], lambda_k2[i], subln[i],
               norm_mix_post[i], norm_mlp_pre[i], wu, wd, norm_mlp_post[i], wp, wg, norm_ple_post[i])
        y_prompt = _layer(y_prompt, p_prompt[i], i, prm)
        y_sample = _layer(y_sample, p_sample[i], i, prm)
    return (y_prompt, y_sample)
```

```python
import functools
import math

import jax
import jax.numpy as jnp
import numpy as np
from jax import lax
from jax.experimental import pallas as pl
from jax.experimental.pallas import tpu as pltpu

D_MODEL = 1024
N_HEADS = 8
QK_HEAD_DIM = 64
V_HEAD_DIM = 128
Q_WIDTH = N_HEADS * 2 * QK_HEAD_DIM
ATTN_WIDTH = N_HEADS * V_HEAD_DIM
N_FOURIER_GROUPS = 4
FOURIER_GROUP_DIM = 128
FOURIER_WIDTH = N_FOURIER_GROUPS * FOURIER_GROUP_DIM
ROPE_DIM = QK_HEAD_DIM // 4
ROPE_THETA = 500000.0
D_FF = 4 * D_MODEL
PLE_DIM = 256
EPS = 1e-6

LANES = 128
VMEM_LIMIT = 56 * 1024 * 1024
F32 = jnp.float32
BF16 = jnp.bfloat16
LOG2E = 1.4426950408889634
ROW_CHUNK = 256


def _rms(x, g):
    return x * lax.rsqrt(jnp.mean(x * x, axis=-1, keepdims=True) + EPS) * g


def _dot(a, b):
    return jnp.dot(a, b, preferred_element_type=F32)


def _sigmoid(x):
    return 1.0 / (1.0 + jnp.exp(-x))


def _const_spec(shape):
    nd = len(shape)
    return pl.BlockSpec(shape, lambda *_: (0,) * nd, pipeline_mode=pl.Buffered(1))


def _rope(y, c, sa, sb):
    outs = []
    for j in range(y.shape[1] // LANES):
        yc = y[:, j * LANES:(j + 1) * LANES]
        outs.append(yc * c + pltpu.roll(yc, ROPE_DIM // 2, 1) * sa
                    + pltpu.roll(yc, LANES - ROPE_DIM // 2, 1) * sb)
    return jnp.concatenate(outs, axis=1)


def _store_heads(ref, rows, val):
    for hd in range(N_HEADS):
        ref[0, hd, rows, :] = val[:, hd * V_HEAD_DIM:(hd + 1) * V_HEAD_DIM]


def _in_proj_kernel(x_ref, g_ref, w_ref, cq_ref, sqa_ref, sqb_ref, ck_ref, ska_ref, skb_ref,
                    fa_ref, fb_ref, q1_ref, q2_ref, k_ref, v_ref, gf_ref, ga_ref):
    comp1 = lax.broadcasted_iota(jnp.int32, (1, Q_WIDTH), 1) % V_HEAD_DIM < QK_HEAD_DIM
    for r in range(x_ref.shape[0] // ROW_CHUNK):
        rows = slice(r * ROW_CHUNK, (r + 1) * ROW_CHUNK)
        h = _rms(x_ref[rows, :], g_ref[...]).astype(BF16)
        o = 0
        fa_ref[rows, :] = _dot(h, w_ref[:, o:o + FOURIER_WIDTH]).astype(BF16); o += FOURIER_WIDTH
        fb_ref[rows, :] = _dot(h, w_ref[:, o:o + FOURIER_WIDTH]).astype(BF16); o += FOURIER_WIDTH
        q = _dot(h, w_ref[:, o:o + Q_WIDTH]); o += Q_WIDTH
        q = _rope(q, cq_ref[rows, :], sqa_ref[rows, :], sqb_ref[rows, :])
        _store_heads(q1_ref, rows, jnp.where(comp1, q, 0.0).astype(BF16))
        _store_heads(q2_ref, rows, jnp.where(comp1, 0.0, q).astype(BF16))
        k = _dot(h, w_ref[:, o:o + Q_WIDTH]); o += Q_WIDTH
        _store_heads(k_ref, rows, _rope(k, ck_ref[rows, :], ska_ref[rows, :], skb_ref[rows, :]).astype(BF16))
        _store_heads(v_ref, rows, _dot(h, w_ref[:, o:o + ATTN_WIDTH]).astype(BF16)); o += ATTN_WIDTH
        gf_ref[rows, :] = _sigmoid(_dot(h, w_ref[:, o:o + D_MODEL])).astype(BF16); o += D_MODEL
        ga_ref[rows, :] = _sigmoid(_dot(h, w_ref[:, o:o + D_MODEL])).astype(BF16)


def _rope_tables(seq, scale):
    half = ROPE_DIM // 2
    pos = jnp.arange(seq, dtype=F32)
    inv_freq = ROPE_THETA ** (-(jnp.arange(0, ROPE_DIM, 2, dtype=F32) / ROPE_DIM))
    ang = pos[:, None] * inv_freq[None, :]
    cos, sin = jnp.cos(ang), jnp.sin(ang)
    ones = jnp.ones((seq, QK_HEAD_DIM - ROPE_DIM), F32)
    zeros = jnp.zeros((seq, QK_HEAD_DIM - ROPE_DIM), F32)
    zh = jnp.zeros((seq, half), F32)
    c = jnp.concatenate([cos, cos, ones], axis=1)
    sa = jnp.concatenate([zh, sin, zeros], axis=1)
    sb = jnp.concatenate([-sin, zh, zeros], axis=1)
    rep = LANES // QK_HEAD_DIM
    return tuple(jnp.tile(t * scale, (1, rep)) for t in (c, sa, sb))


def _in_proj(x2d, gain, w_in, seq, tm):
    n_tok = x2d.shape[0]
    q_scale = LOG2E / math.sqrt(QK_HEAD_DIM)
    tabs = _rope_tables(seq, q_scale) + _rope_tables(seq, 1.0)
    spt = seq // tm
    tok = lambda w: pl.BlockSpec((tm, w), lambda i: (i, 0))
    tab = pl.BlockSpec((tm, LANES), lambda i: (i % spt, 0))
    heads = pl.BlockSpec((1, N_HEADS, tm, V_HEAD_DIM), lambda i: (i // spt, 0, i % spt, 0))
    heads_shape = jax.ShapeDtypeStruct((n_tok // seq, N_HEADS, seq, V_HEAD_DIM), BF16)
    flat = lambda w: (tok(w), jax.ShapeDtypeStruct((n_tok, w), BF16))
    outs = [flat(FOURIER_WIDTH)] * 2 + [(heads, heads_shape)] * 4 + [flat(D_MODEL)] * 2
    return pl.pallas_call(
        _in_proj_kernel,
        grid=(n_tok // tm,),
        in_specs=[tok(D_MODEL), _const_spec((1, D_MODEL)), _const_spec(w_in.shape)] + [tab] * 6,
        out_specs=[spec for spec, _ in outs],
        out_shape=[shape for _, shape in outs],
        compiler_params=pltpu.CompilerParams(dimension_semantics=("parallel",),
                                             vmem_limit_bytes=VMEM_LIMIT),
        name="in_proj",
    )(x2d, gain, w_in, *tabs)


def _chan_dft_weights():
    n = FOURIER_GROUP_DIM
    idx = np.arange(n)
    ang = 2.0 * np.pi * ((idx[:, None] * idx[None, :]) % n) / n
    eye = np.eye(N_FOURIER_GROUPS)
    c = np.kron(eye, np.cos(ang)) / math.sqrt(n)
    s = np.kron(eye, np.sin(ang)) / math.sqrt(n)
    return jnp.asarray(np.concatenate([c, s], axis=1), dtype=BF16)


def _fold_kernel(a_ref, b_ref, o_ref):
    o_ref[...] = _dot(a_ref[...], b_ref[...]).astype(BF16)


def _fold_chan_dft(w_f):
    return pl.pallas_call(
        _fold_kernel,
        out_shape=jax.ShapeDtypeStruct((w_f.shape[0], 2 * FOURIER_WIDTH), BF16),
        compiler_params=pltpu.CompilerParams(vmem_limit_bytes=VMEM_LIMIT),
        name="fold_chan_dft",
    )(w_f, _chan_dft_weights())


def _dense_dft_kernel(a_ref, b_ref, c_ref, s_ref, y_ref):
    y_ref[0] = (_dot(c_ref[...], a_ref[0]) + _dot(s_ref[...], b_ref[0])).astype(BF16)


def _seq_dft_dense(a, b, bsz, seq):
    lo_n = 64
    k = jnp.arange(seq, dtype=jnp.int32)[None, :]

    def cos_sin(rows):
        ang = ((rows[:, None] * k) % seq).astype(F32) * (2.0 * math.pi / seq)
        return jnp.cos(ang), jnp.sin(ang)

    (ch, sh), (cl, sl) = cos_sin(jnp.arange(seq // lo_n, dtype=jnp.int32) * lo_n), cos_sin(
        jnp.arange(lo_n, dtype=jnp.int32))
    sc = 1.0 / math.sqrt(seq)
    c = ((ch[:, None] * cl[None] - sh[:, None] * sl[None]) * sc).astype(BF16).reshape(seq, seq)
    s = ((sh[:, None] * cl[None] + ch[:, None] * sl[None]) * (-sc)).astype(BF16).reshape(seq, seq)
    blk = pl.BlockSpec((1, seq, FOURIER_WIDTH), lambda bi: (bi, 0, 0))
    shp = (bsz, seq, FOURIER_WIDTH)
    y = pl.pallas_call(
        _dense_dft_kernel,
        grid=(bsz,),
        in_specs=[blk, blk, _const_spec(c.shape), _const_spec(s.shape)],
        out_specs=blk,
        out_shape=jax.ShapeDtypeStruct(shp, BF16),
        compiler_params=pltpu.CompilerParams(dimension_semantics=("parallel",),
                                             vmem_limit_bytes=VMEM_LIMIT),
        name="seq_dft_dense",
    )(a.reshape(shp), b.reshape(shp), c, s)
    return y.reshape(bsz * seq, FOURIER_WIDTH)


SEQ_DFT_N2 = 128
DFT_STAGE1_BLOCK_ELEMS = 1 << 19
DENSE_DFT_MAX_SEQ = 2048


def _dft_stage1_kernel(a_ref, b_ref, m_ref, tc_ref, ts_ref, vr_ref, vi_ref):
    n1 = a_ref.shape[1]
    u = _dot(m_ref[...], jnp.concatenate([a_ref[0], b_ref[0]], axis=0))
    ur, ui = u[:n1], u[n1:]
    tc, ts = tc_ref[...], ts_ref[...]
    vr_ref[0] = (ur * tc - ui * ts).astype(BF16)
    vi_ref[0] = (ur * ts + ui * tc).astype(BF16)


def _dft_stage2_kernel(vr_ref, vi_ref, m_ref, y_ref):
    for i in range(vr_ref.shape[1]):
        v = jnp.concatenate([vr_ref[0, i], vi_ref[0, i]], axis=0)
        y_ref[0, :, i * FOURIER_WIDTH:(i + 1) * FOURIER_WIDTH] = _dot(m_ref[...], v).astype(BF16)


def _seq_dft(a, b, bsz, seq, tcols, group):
    n2 = SEQ_DFT_N2
    n1 = seq // n2
    cols = n2 * FOURIER_WIDTH
    j1 = np.arange(n1)
    ang1 = 2.0 * np.pi * ((j1[:, None] * j1[None, :]) % n1) / n1
    c1, s1 = np.cos(ang1), np.sin(ang1)
    m1 = jnp.asarray(np.block([[c1, -s1], [s1, c1]]) / math.sqrt(seq), dtype=BF16)
    j2 = np.arange(n2)
    ang2 = 2.0 * np.pi * ((j2[:, None] * j2[None, :]) % n2) / n2
    m3 = jnp.asarray(np.concatenate([np.cos(ang2), -np.sin(ang2)], axis=1), dtype=BF16)
    angt = (jnp.arange(n1, dtype=F32)[:, None] * jnp.arange(n2, dtype=F32)[None, :]) * (2.0 * math.pi / seq)
    tcos = jnp.repeat(jnp.cos(angt), FOURIER_WIDTH, axis=1)
    tsin = jnp.repeat(jnp.sin(angt), FOURIER_WIDTH, axis=1)

    blk = pl.BlockSpec((1, n1, tcols), lambda c, bi: (bi, 0, c))
    twd = pl.BlockSpec((n1, tcols), lambda c, bi: (0, c))
    vr, vi = pl.pallas_call(
        _dft_stage1_kernel,
        grid=(cols // tcols, bsz),
        in_specs=[blk, blk, _const_spec(m1.shape), twd, twd],
        out_specs=[blk, blk],
        out_shape=[jax.ShapeDtypeStruct((bsz, n1, cols), BF16)] * 2,
        compiler_params=pltpu.CompilerParams(dimension_semantics=("parallel", "parallel"),
                                             vmem_limit_bytes=VMEM_LIMIT),
        name="seq_dft_stage1",
    )(a.reshape(bsz, n1, cols), b.reshape(bsz, n1, cols), m1, tcos, tsin)

    vblk = pl.BlockSpec((1, group, n2, FOURIER_WIDTH), lambda bi, g: (bi, g, 0, 0))
    y = pl.pallas_call(
        _dft_stage2_kernel,
        grid=(bsz, n1 // group),
        in_specs=[vblk, vblk, _const_spec(m3.shape)],
        out_specs=pl.BlockSpec((1, n2, group * FOURIER_WIDTH), lambda bi, g: (bi, 0, g)),
        out_shape=jax.ShapeDtypeStruct((bsz, n2, n1 * FOURIER_WIDTH), BF16),
        compiler_params=pltpu.CompilerParams(dimension_semantics=("parallel", "parallel"),
                                             vmem_limit_bytes=VMEM_LIMIT),
        name="seq_dft_stage2",
    )(vr.reshape(bsz, n1, n2, FOURIER_WIDTH), vi.reshape(bsz, n1, n2, FOURIER_WIDTH), m3)
    return y.reshape(bsz * seq, FOURIER_WIDTH)


def _lambda_kernel(lam_init, q1_ref, k1_ref, q2_ref, k2_ref, o_ref):
    s1 = jnp.sum(q1_ref[...] * k1_ref[...], axis=-1, keepdims=True)
    s2 = jnp.sum(q2_ref[...] * k2_ref[...], axis=-1, keepdims=True)
    o_ref[...] = jnp.exp(s1) - jnp.exp(s2) + lam_init


def _lambda(lam_init, q1, k1, q2, k2):
    vec = lambda a: a.reshape(1, QK_HEAD_DIM).astype(F32)
    return pl.pallas_call(
        functools.partial(_lambda_kernel, lam_init),
        out_shape=jax.ShapeDtypeStruct((1, 1), F32),
        name="diff_lambda",
    )(vec(q1), vec(k1), vec(q2), vec(k2))


ATTN_RESIDENT_MAX_SEQ = 2048


def _attn_kernel(tq, tk, unroll, fin_group, resident, out_scale, lam_ref, q1_ref, q2_ref, k_ref, v_ref, sub_ref, o_ref,
                 s0_ref, s1_ref, p0_ref, p1_ref, a0_ref, a1_ref, m0_ref, m1_ref, acc_ref):
    s_refs, p_refs, a_refs, m_refs = (s0_ref, s1_ref), (p0_ref, p1_ref), (a0_ref, a1_ref), (m0_ref, m1_ref)
    seq = k_ref.shape[2]
    head = pl.program_id(1)
    nq, nk = seq // tq, seq // tk
    n_steps = nq * nk
    assert nk & (nk - 1) == 0 and nq % fin_group == 0
    assert unroll % 2 == 0 and n_steps % unroll == 0
    nk_shift = nk.bit_length() - 1
    lam = lam_ref[0, 0]
    nt = (((1,), (1,)), ((), ()))
    ones = jnp.ones((tk, V_HEAD_DIM), BF16)

    def split(t):
        return lax.shift_right_logical(t, nk_shift), lax.bitwise_and(t, nk - 1)

    def scores(t, slot, hd):
        qi, kj = split(t)
        hd = hd if resident else 0
        rows = pl.ds(pl.multiple_of(qi * tq, tq), tq)
        qz = jnp.concatenate([q1_ref[0, hd, rows, :], q2_ref[0, hd, rows, :]], axis=0)
        kt = k_ref[0, hd, pl.ds(pl.multiple_of(kj * tk, tk), tk), :]
        s_refs[slot][...] = lax.dot_general(qz, kt, nt, preferred_element_type=F32)

    def softmax(t, slot, can_be_first):
        _, kj = split(t)
        s_ref, m_in, m_out = s_refs[slot], m_refs[slot], m_refs[1 - slot]
        for g in range(2 * tq // 8):
            rows = slice(g * 8, (g + 1) * 8)
            blocks = [s_ref[rows, j * LANES:(j + 1) * LANES] for j in range(tk // LANES)]
            m_old = m_in[rows, :]
            if can_be_first:
                m_old = jnp.where(kj == 0, -jnp.inf, m_old)
            blk_max = functools.reduce(jnp.maximum, blocks)
            m_new = jnp.maximum(m_old, jnp.max(blk_max, axis=-1, keepdims=True))
            m_out[rows, :] = m_new
            a_refs[slot][rows, :] = jnp.exp2(m_old - m_new)
            p_refs[slot][rows, :] = jnp.concatenate(
                [jnp.exp2(b - m_new) for b in blocks], axis=1).astype(BF16)

    def pv(u, slot, tile, can_be_first):
        _, kj = split(u)
        vt = v_ref[0, head if resident else 0, pl.ds(pl.multiple_of(kj * tk, tk), tk), :]
        v1 = jnp.concatenate([vt, ones], axis=1)
        alpha = a_refs[slot][...]
        alpha2 = jnp.concatenate([alpha, alpha], axis=1)
        acc = acc_ref[tile]
        if can_be_first:
            acc = jnp.where(kj == 0, 0.0, acc)
        acc_ref[tile] = alpha2 * acc + _dot(p_refs[slot][...], v1)

    @pl.when((pl.program_id(0) == 0) & (pl.program_id(1) == 0))
    def _():
        acc_ref[...] = jnp.zeros(acc_ref.shape, F32)
        m0_ref[...] = jnp.zeros(m0_ref.shape, F32)

    if resident:
        pl.when(head == 0)(lambda: scores(0, 0, 0))
    else:
        scores(0, 0, head)

    def body(i, _):
        t = unroll * i
        for j in range(unroll):
            nxt = t + j + 1
            if j < unroll - 1:
                scores(nxt, (j + 1) % 2, head)
            elif resident:
                wrap = nxt >= n_steps
                scores(jnp.where(wrap, 0, nxt), 0, jnp.where(wrap, jnp.minimum(head + 1, N_HEADS - 1), head))
            else:
                scores(jnp.minimum(nxt, n_steps - 1), 0, head)
            first = j % math.gcd(unroll, nk) == 0
            softmax(t + j, j % 2, first)
            pv(t + j, j % 2, split(t + j)[0], first)
        return 0

    lax.fori_loop(0, n_steps // unroll, body, 0)

    sub = sub_ref[...] * out_scale

    def finish(i, _):
        for j in range(fin_group):
            tile = i * fin_group + j
            a = acc_ref[tile]
            o = a[:, :V_HEAD_DIM] / a[:, V_HEAD_DIM:]
            att = o[:tq] - lam * o[tq:]
            o_ref[0, pl.ds(pl.multiple_of(tile * tq, tq), tq), :] = _rms(att, sub).astype(BF16)
        return 0

    lax.fori_loop(0, nq // fin_group, finish, 0)


def _diff_attn(lam, q1, q2, k, v, subln, out_scale, tq, tk, unroll, fin_group):
    b, _, seq, _ = k.shape
    resident = seq <= ATTN_RESIDENT_MAX_SEQ
    if resident:
        head = pl.BlockSpec((1, N_HEADS, seq, V_HEAD_DIM), lambda bi, h: (bi, 0, 0, 0))
    else:
        head = pl.BlockSpec((1, 1, seq, V_HEAD_DIM), lambda bi, h: (bi, h, 0, 0))
    return pl.pallas_call(
        functools.partial(_attn_kernel, tq, tk, unroll, fin_group, resident, out_scale),
        grid=(b, N_HEADS),
        in_specs=[pl.BlockSpec(memory_space=pltpu.SMEM), head, head, head, head,
                  _const_spec((1, V_HEAD_DIM))],
        out_specs=pl.BlockSpec((1, seq, V_HEAD_DIM), lambda bi, h: (bi, 0, h)),
        out_shape=jax.ShapeDtypeStruct((b, seq, ATTN_WIDTH), BF16),
        scratch_shapes=[pltpu.VMEM((2 * tq, tk), F32)] * 2 + [pltpu.VMEM((2 * tq, tk), BF16)] * 2
        + [pltpu.VMEM((2 * tq, LANES), F32)] * 4
        + [pltpu.VMEM((seq // tq, 2 * tq, 2 * V_HEAD_DIM), F32)],
        compiler_params=pltpu.CompilerParams(dimension_semantics=("arbitrary", "arbitrary"),
                                             vmem_limit_bytes=VMEM_LIMIT),
        name="diff_attn",
    )(lam, q1, q2, k, v, subln)


def _merge_kernel(x_ref, y_ref, att_ref, gf_ref, ga_ref, wf_ref, wa_ref, wo_ref, g_ref, o_ref):
    fo = _dot(y_ref[...], wf_ref[...])
    ao = _dot(att_ref[...], wa_ref[...])
    merged = gf_ref[...].astype(F32) * fo + ga_ref[...].astype(F32) * ao
    o_ref[...] = x_ref[...] + _rms(_dot(merged.astype(BF16), wo_ref[...]), g_ref[...])


def _merge(x2d, y2, att2d, gf, ga, w_fourier, w_attn, w_out, gain, tm):
    n_tok = x2d.shape[0]
    tok = lambda w: pl.BlockSpec((tm, w), lambda i: (i, 0))
    return pl.pallas_call(
        _merge_kernel,
        grid=(n_tok // tm,),
        in_specs=[tok(D_MODEL), tok(FOURIER_WIDTH), tok(ATTN_WIDTH), tok(D_MODEL), tok(D_MODEL),
                  _const_spec(w_fourier.shape), _const_spec(w_attn.shape),
                  _const_spec(w_out.shape), _const_spec((1, D_MODEL))],
        out_specs=tok(D_MODEL),
        out_shape=jax.ShapeDtypeStruct((n_tok, D_MODEL), F32),
        compiler_params=pltpu.CompilerParams(dimension_semantics=("parallel",),
                                             vmem_limit_bytes=VMEM_LIMIT),
        name="merge",
    )(x2d, y2, att2d, gf, ga, w_fourier, w_attn, w_out, gain)


def _mlp_ple_kernel(x_ref, p_ref, gpre_ref, wup_ref, wdn_ref, gpost_ref, wple_ref, wgate_ref,
                    gple_ref, o_ref):
    for r in range(x_ref.shape[0] // ROW_CHUNK):
        rows = slice(r * ROW_CHUNK, (r + 1) * ROW_CHUNK)
        x = x_ref[rows, :]
        h = _rms(x, gpre_ref[...]).astype(BF16)
        d = jnp.zeros(x.shape, F32)
        for c in range(D_FF // D_MODEL):
            cols = slice(c * D_MODEL, (c + 1) * D_MODEL)
            u = jnp.square(jnp.maximum(_dot(h, wup_ref[:, cols]), 0.0))
            d = d + _dot(u.astype(BF16), wdn_ref[cols, :])
        x = x + _rms(d, gpost_ref[...])
        e = _dot(p_ref[rows, :].astype(BF16), wple_ref[...]) * _sigmoid(_dot(x.astype(BF16), wgate_ref[...]))
        o_ref[rows, :] = x + _rms(e, gple_ref[...])


def _mlp_ple(x2d, p2d, g_pre, w_up, w_down, g_post, w_ple, w_gate, g_ple, tm):
    n_tok = x2d.shape[0]
    tok = lambda w: pl.BlockSpec((tm, w), lambda i: (i, 0))
    vec = _const_spec((1, D_MODEL))
    return pl.pallas_call(
        _mlp_ple_kernel,
        grid=(n_tok // tm,),
        in_specs=[tok(D_MODEL), tok(PLE_DIM), vec, _const_spec(w_up.shape),
                  _const_spec(w_down.shape), vec, _const_spec(w_ple.shape),
                  _const_spec(w_gate.shape), vec],
        out_specs=tok(D_MODEL),
        out_shape=jax.ShapeDtypeStruct((n_tok, D_MODEL), F32),
        compiler_params=pltpu.CompilerParams(dimension_semantics=("parallel",),
                                             vmem_limit_bytes=VMEM_LIMIT),
        name="mlp_ple",
    )(x2d, p2d, g_pre, w_up, w_down, g_post, w_ple, w_gate, g_ple)


def _layer(x, p, layer_idx, prm):
    (norm_mix_pre, w_in, w_fourier, w_attn, w_out, lq1, lk1, lq2, lk2, subln,
     norm_mix_post, norm_mlp_pre, w_up, w_down, norm_mlp_post, w_ple, w_gate, norm_ple_post) = prm
    b, seq, _ = x.shape
    n_tok = b * seq
    row = lambda g: g.reshape(1, -1).astype(F32)
    x2d = x.reshape(n_tok, D_MODEL)

    fa, fb, q1, q2, k, v, gf, ga = _in_proj(x2d, row(norm_mix_pre), w_in, seq, tm=512)
    if seq <= DENSE_DFT_MAX_SEQ:
        y2 = _seq_dft_dense(fa, fb, b, seq)
    else:
        y2 = _seq_dft(fa, fb, b, seq, tcols=DFT_STAGE1_BLOCK_ELEMS * SEQ_DFT_N2 // seq, group=8)

    lam_init = 0.8 - 0.6 * math.exp(-0.3 * layer_idx)
    lam = _lambda(lam_init, lq1, lk1, lq2, lk2)
    att = _diff_attn(lam, q1, q2, k, v, row(subln),
                     1.0 - lam_init, tq=512, tk=1024, unroll=4, fin_group=4)

    x1 = _merge(x2d, y2, att.reshape(n_tok, ATTN_WIDTH), gf, ga, w_fourier, w_attn, w_out,
                row(norm_mix_post), tm=1024)
    out = _mlp_ple(x1, p.reshape(n_tok, PLE_DIM), row(norm_mlp_pre), w_up, w_down,
                   row(norm_mlp_post), w_ple, w_gate, row(norm_ple_post), tm=1024)
    return out.reshape(b, seq, D_MODEL)


def _prepare_weights(w_in, w_fourier, w_attn, w_out, w_up, w_down, w_ple, w_gate):
    w_in = w_in.astype(BF16)
    w_in = jnp.concatenate([_fold_chan_dft(w_in[:, :FOURIER_WIDTH]), w_in[:, FOURIER_WIDTH:]], axis=1)
    return (w_in,) + tuple(w.astype(BF16) for w in (w_fourier, w_attn, w_out, w_up, w_down, w_ple, w_gate))


def kernel(x_prompt, x_sample, p_prompt, p_sample, norm_mix_pre, w_in, w_fourier, w_attn, w_out,
           lambda_q1, lambda_k1, lambda_q2, lambda_k2, subln, norm_mix_post, norm_mlp_pre,
           w_up, w_down, norm_mlp_post, w_ple, w_ple_gate, norm_ple_post):
    y_prompt, y_sample = x_prompt, x_sample
    for i in range(w_in.shape[0]):
        wi, wf, wa, wo, wu, wd, wp, wg = _prepare_weights(
            w_in[i], w_fourier[i], w_attn[i], w_out[i], w_up[i], w_down[i], w_ple[i], w_ple_gate[i])
        prm = (norm_mix_pre[i], wi, wf, wa, wo,
               lambda_q1[i], lambda_k1[i], lambda_q2[i], lambda_k2[i], subln[i],
               norm_mix_post[i], norm_mlp_pre[i], wu, wd, norm_mlp_post[i], wp, wg, norm_ple_post[i])
        y_prompt = _layer(y_prompt, p_prompt[i], i, prm)
        y_sample = _layer(y_sample, p_sample[i], i, prm)
    return (y_prompt, y_sample)
```

```python
import functools
import math

import jax
import jax.numpy as jnp
import numpy as np
from jax import lax
from jax.experimental import pallas as pl
from jax.experimental.pallas import tpu as pltpu

D_MODEL = 1024
N_HEADS = 8
QK_HEAD_DIM = 64
V_HEAD_DIM = 128
Q_WIDTH = N_HEADS * 2 * QK_HEAD_DIM
ATTN_WIDTH = N_HEADS * V_HEAD_DIM
N_FOURIER_GROUPS = 4
FOURIER_GROUP_DIM = 128
FOURIER_WIDTH = N_FOURIER_GROUPS * FOURIER_GROUP_DIM
ROPE_DIM = QK_HEAD_DIM // 4
ROPE_THETA = 500000.0
D_FF = 4 * D_MODEL
PLE_DIM = 256
EPS = 1e-6

LANES = 128
VMEM_LIMIT = 56 * 1024 * 1024
F32 = jnp.float32
BF16 = jnp.bfloat16
LOG2E = 1.4426950408889634
ROW_CHUNK = 256


def _rms(x, g):
    return x * lax.rsqrt(jnp.mean(x * x, axis=-1, keepdims=True) + EPS) * g


def _dot(a, b):
    return jnp.dot(a, b, preferred_element_type=F32)


def _sigmoid(x):
    return 1.0 / (1.0 + jnp.exp(-x))


def _const_spec(shape):
    nd = len(shape)
    return pl.BlockSpec(shape, lambda *_: (0,) * nd, pipeline_mode=pl.Buffered(1))


def _rope(y, c, sa, sb):
    outs = []
    for j in range(y.shape[1] // LANES):
        yc = y[:, j * LANES:(j + 1) * LANES]
        outs.append(yc * c + pltpu.roll(yc, ROPE_DIM // 2, 1) * sa
                    + pltpu.roll(yc, LANES - ROPE_DIM // 2, 1) * sb)
    return jnp.concatenate(outs, axis=1)


def _store_heads(ref, rows, val):
    for hd in range(N_HEADS):
        ref[0, hd, rows, :] = val[:, hd * V_HEAD_DIM:(hd + 1) * V_HEAD_DIM]


def _in_proj_kernel(x_ref, g_ref, wfold_ref, w_ref, cq_ref, sqa_ref, sqb_ref, ck_ref, ska_ref, skb_ref,
                    fa_ref, fb_ref, q1_ref, q2_ref, k_ref, v_ref, gf_ref, ga_ref):
    comp1 = lax.broadcasted_iota(jnp.int32, (1, Q_WIDTH), 1) % V_HEAD_DIM < QK_HEAD_DIM
    for r in range(x_ref.shape[0] // ROW_CHUNK):
        rows = slice(r * ROW_CHUNK, (r + 1) * ROW_CHUNK)
        h = _rms(x_ref[rows, :], g_ref[...]).astype(BF16)
        fa_ref[rows, :] = _dot(h, wfold_ref[:, :FOURIER_WIDTH]).astype(BF16)
        fb_ref[rows, :] = _dot(h, wfold_ref[:, FOURIER_WIDTH:]).astype(BF16)
        o = FOURIER_WIDTH
        q = _dot(h, w_ref[:, o:o + Q_WIDTH]); o += Q_WIDTH
        q = _rope(q, cq_ref[rows, :], sqa_ref[rows, :], sqb_ref[rows, :])
        _store_heads(q1_ref, rows, jnp.where(comp1, q, 0.0).astype(BF16))
        _store_heads(q2_ref, rows, jnp.where(comp1, 0.0, q).astype(BF16))
        k = _dot(h, w_ref[:, o:o + Q_WIDTH]); o += Q_WIDTH
        _store_heads(k_ref, rows, _rope(k, ck_ref[rows, :], ska_ref[rows, :], skb_ref[rows, :]).astype(BF16))
        _store_heads(v_ref, rows, _dot(h, w_ref[:, o:o + ATTN_WIDTH]).astype(BF16)); o += ATTN_WIDTH
        gf_ref[rows, :] = _sigmoid(_dot(h, w_ref[:, o:o + D_MODEL])).astype(BF16); o += D_MODEL
        ga_ref[rows, :] = _sigmoid(_dot(h, w_ref[:, o:o + D_MODEL])).astype(BF16)


def _rope_tables(seq, scale):
    half = ROPE_DIM // 2
    pos = jnp.arange(seq, dtype=F32)
    inv_freq = ROPE_THETA ** (-(jnp.arange(0, ROPE_DIM, 2, dtype=F32) / ROPE_DIM))
    ang = pos[:, None] * inv_freq[None, :]
    cos, sin = jnp.cos(ang), jnp.sin(ang)
    ones = jnp.ones((seq, QK_HEAD_DIM - ROPE_DIM), F32)
    zeros = jnp.zeros((seq, QK_HEAD_DIM - ROPE_DIM), F32)
    zh = jnp.zeros((seq, half), F32)
    c = jnp.concatenate([cos, cos, ones], axis=1)
    sa = jnp.concatenate([zh, sin, zeros], axis=1)
    sb = jnp.concatenate([-sin, zh, zeros], axis=1)
    rep = LANES // QK_HEAD_DIM
    return tuple(jnp.tile(t * scale, (1, rep)) for t in (c, sa, sb))


def _in_proj(x2d, gain, w_fold, w_in, seq, tm):
    n_tok = x2d.shape[0]
    q_scale = LOG2E / math.sqrt(QK_HEAD_DIM)
    tabs = _rope_tables(seq, q_scale) + _rope_tables(seq, 1.0)
    spt = seq // tm
    tok = lambda w: pl.BlockSpec((tm, w), lambda i: (i, 0))
    tab = pl.BlockSpec((tm, LANES), lambda i: (i % spt, 0))
    heads = pl.BlockSpec((1, N_HEADS, tm, V_HEAD_DIM), lambda i: (i // spt, 0, i % spt, 0))
    heads_shape = jax.ShapeDtypeStruct((n_tok // seq, N_HEADS, seq, V_HEAD_DIM), BF16)
    flat = lambda w: (tok(w), jax.ShapeDtypeStruct((n_tok, w), BF16))
    outs = [flat(FOURIER_WIDTH)] * 2 + [(heads, heads_shape)] * 4 + [flat(D_MODEL)] * 2
    return pl.pallas_call(
        _in_proj_kernel,
        grid=(n_tok // tm,),
        in_specs=[tok(D_MODEL), _const_spec((1, D_MODEL)), _const_spec(w_fold.shape),
                  _const_spec(w_in.shape)] + [tab] * 6,
        out_specs=[spec for spec, _ in outs],
        out_shape=[shape for _, shape in outs],
        compiler_params=pltpu.CompilerParams(dimension_semantics=("parallel",),
                                             vmem_limit_bytes=VMEM_LIMIT),
        name="in_proj",
    )(x2d, gain, w_fold, w_in, *tabs)


def _chan_dft_weights():
    n = FOURIER_GROUP_DIM
    idx = np.arange(n)
    ang = 2.0 * np.pi * ((idx[:, None] * idx[None, :]) % n) / n
    eye = np.eye(N_FOURIER_GROUPS)
    c = np.kron(eye, np.cos(ang)) / math.sqrt(n)
    s = np.kron(eye, np.sin(ang)) / math.sqrt(n)
    return jnp.asarray(np.concatenate([c, s], axis=1), dtype=BF16)


def _fold_kernel(a_ref, b_ref, o_ref):
    o_ref[...] = _dot(a_ref[...], b_ref[...]).astype(BF16)


def _fold_chan_dft(w_f):
    return pl.pallas_call(
        _fold_kernel,
        out_shape=jax.ShapeDtypeStruct((w_f.shape[0], 2 * FOURIER_WIDTH), BF16),
        compiler_params=pltpu.CompilerParams(vmem_limit_bytes=VMEM_LIMIT),
        name="fold_chan_dft",
    )(w_f, _chan_dft_weights())


def _dense_dft_kernel(a_ref, b_ref, c_ref, s_ref, y_ref):
    y_ref[0] = (_dot(c_ref[...], a_ref[0]) + _dot(s_ref[...], b_ref[0])).astype(BF16)


def _seq_dft_dense(a, b, bsz, seq):
    lo_n = 64
    k = jnp.arange(seq, dtype=jnp.int32)[None, :]

    def cos_sin(rows):
        ang = ((rows[:, None] * k) % seq).astype(F32) * (2.0 * math.pi / seq)
        return jnp.cos(ang), jnp.sin(ang)

    (ch, sh), (cl, sl) = cos_sin(jnp.arange(seq // lo_n, dtype=jnp.int32) * lo_n), cos_sin(
        jnp.arange(lo_n, dtype=jnp.int32))
    sc = 1.0 / math.sqrt(seq)
    c = ((ch[:, None] * cl[None] - sh[:, None] * sl[None]) * sc).astype(BF16).reshape(seq, seq)
    s = ((sh[:, None] * cl[None] + ch[:, None] * sl[None]) * (-sc)).astype(BF16).reshape(seq, seq)
    blk = pl.BlockSpec((1, seq, FOURIER_WIDTH), lambda bi: (bi, 0, 0))
    shp = (bsz, seq, FOURIER_WIDTH)
    y = pl.pallas_call(
        _dense_dft_kernel,
        grid=(bsz,),
        in_specs=[blk, blk, _const_spec(c.shape), _const_spec(s.shape)],
        out_specs=blk,
        out_shape=jax.ShapeDtypeStruct(shp, BF16),
        compiler_params=pltpu.CompilerParams(dimension_semantics=("parallel",),
                                             vmem_limit_bytes=VMEM_LIMIT),
        name="seq_dft_dense",
    )(a.reshape(shp), b.reshape(shp), c, s)
    return y.reshape(bsz * seq, FOURIER_WIDTH)


SEQ_DFT_N2 = 128
DFT_STAGE1_BLOCK_ELEMS = 1 << 19
DENSE_DFT_MAX_SEQ = 2048


def _dft_stage1_kernel(a_ref, b_ref, m_ref, tc_ref, ts_ref, vr_ref, vi_ref):
    n1 = a_ref.shape[1]
    u = _dot(m_ref[...], jnp.concatenate([a_ref[0], b_ref[0]], axis=0))
    ur, ui = u[:n1], u[n1:]
    tc, ts = tc_ref[...], ts_ref[...]
    vr_ref[0] = (ur * tc - ui * ts).astype(BF16)
    vi_ref[0] = (ur * ts + ui * tc).astype(BF16)


def _dft_stage2_kernel(vr_ref, vi_ref, m_ref, y_ref):
    for i in range(vr_ref.shape[1]):
        v = jnp.concatenate([vr_ref[0, i], vi_ref[0, i]], axis=0)
        y_ref[0, :, i * FOURIER_WIDTH:(i + 1) * FOURIER_WIDTH] = _dot(m_ref[...], v).astype(BF16)


def _seq_dft(a, b, bsz, seq, tcols, group):
    n2 = SEQ_DFT_N2
    n1 = seq // n2
    cols = n2 * FOURIER_WIDTH
    j1 = np.arange(n1)
    ang1 = 2.0 * np.pi * ((j1[:, None] * j1[None, :]) % n1) / n1
    c1, s1 = np.cos(ang1), np.sin(ang1)
    m1 = jnp.asarray(np.block([[c1, -s1], [s1, c1]]) / math.sqrt(seq), dtype=BF16)
    j2 = np.arange(n2)
    ang2 = 2.0 * np.pi * ((j2[:, None] * j2[None, :]) % n2) / n2
    m3 = jnp.asarray(np.concatenate([np.cos(ang2), -np.sin(ang2)], axis=1), dtype=BF16)
    angt = (jnp.arange(n1, dtype=F32)[:, None] * jnp.arange(n2, dtype=F32)[None, :]) * (2.0 * math.pi / seq)
    tcos = jnp.repeat(jnp.cos(angt), FOURIER_WIDTH, axis=1)
    tsin = jnp.repeat(jnp.sin(angt), FOURIER_WIDTH, axis=1)

    blk = pl.BlockSpec((1, n1, tcols), lambda c, bi: (bi, 0, c))
    twd = pl.BlockSpec((n1, tcols), lambda c, bi: (0, c))
    vr, vi = pl.pallas_call(
        _dft_stage1_kernel,
        grid=(cols // tcols, bsz),
        in_specs=[blk, blk, _const_spec(m1.shape), twd, twd],
        out_specs=[blk, blk],
        out_shape=[jax.ShapeDtypeStruct((bsz, n1, cols), BF16)] * 2,
        compiler_params=pltpu.CompilerParams(dimension_semantics=("parallel", "parallel"),
                                             vmem_limit_bytes=VMEM_LIMIT),
        name="seq_dft_stage1",
    )(a.reshape(bsz, n1, cols), b.reshape(bsz, n1, cols), m1, tcos, tsin)

    vblk = pl.BlockSpec((1, group, n2, FOURIER_WIDTH), lambda bi, g: (bi, g, 0, 0))
    y = pl.pallas_call(
        _dft_stage2_kernel,
        grid=(bsz, n1 // group),
        in_specs=[vblk, vblk, _const_spec(m3.shape)],
        out_specs=pl.BlockSpec((1, n2, group * FOURIER_WIDTH), lambda bi, g: (bi, 0, g)),
        out_shape=jax.ShapeDtypeStruct((bsz, n2, n1 * FOURIER_WIDTH), BF16),
        compiler_params=pltpu.CompilerParams(dimension_semantics=("parallel", "parallel"),
                                             vmem_limit_bytes=VMEM_LIMIT),
        name="seq_dft_stage2",
    )(vr.reshape(bsz, n1, n2, FOURIER_WIDTH), vi.reshape(bsz, n1, n2, FOURIER_WIDTH), m3)
    return y.reshape(bsz * seq, FOURIER_WIDTH)


def _lambda_kernel(lam_init, q1_ref, k1_ref, q2_ref, k2_ref, o_ref):
    s1 = jnp.sum(q1_ref[...] * k1_ref[...], axis=-1, keepdims=True)
    s2 = jnp.sum(q2_ref[...] * k2_ref[...], axis=-1, keepdims=True)
    o_ref[...] = jnp.exp(s1) - jnp.exp(s2) + lam_init


def _lambda(lam_init, q1, k1, q2, k2):
    vec = lambda a: a.reshape(1, QK_HEAD_DIM).astype(F32)
    return pl.pallas_call(
        functools.partial(_lambda_kernel, lam_init),
        out_shape=jax.ShapeDtypeStruct((1, 1), F32),
        name="diff_lambda",
    )(vec(q1), vec(k1), vec(q2), vec(k2))


ATTN_RESIDENT_MAX_SEQ = 2048


def _attn_kernel(tq, tk, unroll, fin_group, resident, out_scale, lam_ref, q1_ref, q2_ref, k_ref, v_ref, sub_ref, o_ref,
                 s0_ref, s1_ref, p0_ref, p1_ref, a0_ref, a1_ref, m0_ref, m1_ref, acc_ref):
    s_refs, p_refs, a_refs, m_refs = (s0_ref, s1_ref), (p0_ref, p1_ref), (a0_ref, a1_ref), (m0_ref, m1_ref)
    seq = k_ref.shape[2]
    head = pl.program_id(1)
    nq, nk = seq // tq, seq // tk
    n_steps = nq * nk
    assert nk & (nk - 1) == 0 and nq % fin_group == 0
    assert unroll % 2 == 0 and n_steps % unroll == 0
    nk_shift = nk.bit_length() - 1
    lam = lam_ref[0, 0]
    nt = (((1,), (1,)), ((), ()))
    ones = jnp.ones((tk, V_HEAD_DIM), BF16)

    def split(t):
        return lax.shift_right_logical(t, nk_shift), lax.bitwise_and(t, nk - 1)

    def scores(t, slot, hd):
        qi, kj = split(t)
        hd = hd if resident else 0
        rows = pl.ds(pl.multiple_of(qi * tq, tq), tq)
        qz = jnp.concatenate([q1_ref[0, hd, rows, :], q2_ref[0, hd, rows, :]], axis=0)
        kt = k_ref[0, hd, pl.ds(pl.multiple_of(kj * tk, tk), tk), :]
        s_refs[slot][...] = lax.dot_general(qz, kt, nt, preferred_element_type=F32)

    def softmax(t, slot, can_be_first):
        _, kj = split(t)
        s_ref, m_in, m_out = s_refs[slot], m_refs[slot], m_refs[1 - slot]
        for g in range(2 * tq // 8):
            rows = slice(g * 8, (g + 1) * 8)
            blocks = [s_ref[rows, j * LANES:(j + 1) * LANES] for j in range(tk // LANES)]
            m_old = m_in[rows, :]
            if can_be_first:
                m_old = jnp.where(kj == 0, -jnp.inf, m_old)
            blk_max = functools.reduce(jnp.maximum, blocks)
            m_new = jnp.maximum(m_old, jnp.max(blk_max, axis=-1, keepdims=True))
            m_out[rows, :] = m_new
            a_refs[slot][rows, :] = jnp.exp2(m_old - m_new)
            p_refs[slot][rows, :] = jnp.concatenate(
                [jnp.exp2(b - m_new) for b in blocks], axis=1).astype(BF16)

    def pv(u, slot, tile, can_be_first):
        _, kj = split(u)
        vt = v_ref[0, head if resident else 0, pl.ds(pl.multiple_of(kj * tk, tk), tk), :]
        v1 = jnp.concatenate([vt, ones], axis=1)
        alpha = a_refs[slot][...]
        alpha2 = jnp.concatenate([alpha, alpha], axis=1)
        acc = acc_ref[tile]
        if can_be_first:
            acc = jnp.where(kj == 0, 0.0, acc)
        acc_ref[tile] = alpha2 * acc + _dot(p_refs[slot][...], v1)

    @pl.when((pl.program_id(0) == 0) & (pl.program_id(1) == 0))
    def _():
        acc_ref[...] = jnp.zeros(acc_ref.shape, F32)
        m0_ref[...] = jnp.zeros(m0_ref.shape, F32)

    if resident:
        pl.when(head == 0)(lambda: scores(0, 0, 0))
    else:
        scores(0, 0, head)

    def body(i, _):
        t = unroll * i
        for j in range(unroll):
            nxt = t + j + 1
            if j < unroll - 1:
                scores(nxt, (j + 1) % 2, head)
            elif resident:
                wrap = nxt >= n_steps
                scores(jnp.where(wrap, 0, nxt), 0, jnp.where(wrap, jnp.minimum(head + 1, N_HEADS - 1), head))
            else:
                scores(jnp.minimum(nxt, n_steps - 1), 0, head)
            first = j % math.gcd(unroll, nk) == 0
            softmax(t + j, j % 2, first)
            pv(t + j, j % 2, split(t + j)[0], first)
        return 0

    lax.fori_loop(0, n_steps // unroll, body, 0)

    sub = sub_ref[...] * out_scale

    def finish(i, _):
        for j in range(fin_group):
            tile = i * fin_group + j
            a = acc_ref[tile]
            o = a[:, :V_HEAD_DIM] / a[:, V_HEAD_DIM:]
            att = o[:tq] - lam * o[tq:]
            o_ref[0, pl.ds(pl.multiple_of(tile * tq, tq), tq), :] = _rms(att, sub).astype(BF16)
        return 0

    lax.fori_loop(0, nq // fin_group, finish, 0)


def _diff_attn(lam, q1, q2, k, v, subln, out_scale, tq, tk, unroll, fin_group):
    b, _, seq, _ = k.shape
    resident = seq <= ATTN_RESIDENT_MAX_SEQ
    if resident:
        head = pl.BlockSpec((1, N_HEADS, seq, V_HEAD_DIM), lambda bi, h: (bi, 0, 0, 0))
    else:
        head = pl.BlockSpec((1, 1, seq, V_HEAD_DIM), lambda bi, h: (bi, h, 0, 0))
    return pl.pallas_call(
        functools.partial(_attn_kernel, tq, tk, unroll, fin_group, resident, out_scale),
        grid=(b, N_HEADS),
        in_specs=[pl.BlockSpec(memory_space=pltpu.SMEM), head, head, head, head,
                  _const_spec((1, V_HEAD_DIM))],
        out_specs=pl.BlockSpec((1, seq, V_HEAD_DIM), lambda bi, h: (bi, 0, h)),
        out_shape=jax.ShapeDtypeStruct((b, seq, ATTN_WIDTH), BF16),
        scratch_shapes=[pltpu.VMEM((2 * tq, tk), F32)] * 2 + [pltpu.VMEM((2 * tq, tk), BF16)] * 2
        + [pltpu.VMEM((2 * tq, LANES), F32)] * 4
        + [pltpu.VMEM((seq // tq, 2 * tq, 2 * V_HEAD_DIM), F32)],
        compiler_params=pltpu.CompilerParams(dimension_semantics=("arbitrary", "arbitrary"),
                                             vmem_limit_bytes=VMEM_LIMIT),
        name="diff_attn",
    )(lam, q1, q2, k, v, subln)


def _merge_kernel(x_ref, y_ref, att_ref, gf_ref, ga_ref, wf_ref, wa_ref, wo_ref, g_ref, o_ref):
    fo = _dot(y_ref[...], wf_ref[...])
    ao = _dot(att_ref[...], wa_ref[...])
    merged = gf_ref[...].astype(F32) * fo + ga_ref[...].astype(F32) * ao
    o_ref[...] = x_ref[...] + _rms(_dot(merged.astype(BF16), wo_ref[...]), g_ref[...])


def _merge(x2d, y2, att2d, gf, ga, w_fourier, w_attn, w_out, gain, tm):
    n_tok = x2d.shape[0]
    tok = lambda w: pl.BlockSpec((tm, w), lambda i: (i, 0))
    return pl.pallas_call(
        _merge_kernel,
        grid=(n_tok // tm,),
        in_specs=[tok(D_MODEL), tok(FOURIER_WIDTH), tok(ATTN_WIDTH), tok(D_MODEL), tok(D_MODEL),
                  _const_spec(w_fourier.shape), _const_spec(w_attn.shape),
                  _const_spec(w_out.shape), _const_spec((1, D_MODEL))],
        out_specs=tok(D_MODEL),
        out_shape=jax.ShapeDtypeStruct((n_tok, D_MODEL), F32),
        compiler_params=pltpu.CompilerParams(dimension_semantics=("parallel",),
                                             vmem_limit_bytes=VMEM_LIMIT),
        name="merge",
    )(x2d, y2, att2d, gf, ga, w_fourier, w_attn, w_out, gain)


def _mlp_ple_kernel(x_ref, p_ref, gpre_ref, wup_ref, wdn_ref, gpost_ref, wple_ref, wgate_ref,
                    gple_ref, o_ref):
    for r in range(x_ref.shape[0] // ROW_CHUNK):
        rows = slice(r * ROW_CHUNK, (r + 1) * ROW_CHUNK)
        x = x_ref[rows, :]
        h = _rms(x, gpre_ref[...]).astype(BF16)
        d = jnp.zeros(x.shape, F32)
        for c in range(D_FF // D_MODEL):
            cols = slice(c * D_MODEL, (c + 1) * D_MODEL)
            u = jnp.square(jnp.maximum(_dot(h, wup_ref[:, cols]), 0.0))
            d = d + _dot(u.astype(BF16), wdn_ref[cols, :])
        x = x + _rms(d, gpost_ref[...])
        e = _dot(p_ref[rows, :].astype(BF16), wple_ref[...]) * _sigmoid(_dot(x.astype(BF16), wgate_ref[...]))
        o_ref[rows, :] = x + _rms(e, gple_ref[...])


def _mlp_ple(x2d, p2d, g_pre, w_up, w_down, g_post, w_ple, w_gate, g_ple, tm):
    n_tok = x2d.shape[0]
    tok = lambda w: pl.BlockSpec((tm, w), lambda i: (i, 0))
    vec = _const_spec((1, D_MODEL))
    return pl.pallas_call(
        _mlp_ple_kernel,
        grid=(n_tok // tm,),
        in_specs=[tok(D_MODEL), tok(PLE_DIM), vec, _const_spec(w_up.shape),
                  _const_spec(w_down.shape), vec, _const_spec(w_ple.shape),
                  _const_spec(w_gate.shape), vec],
        out_specs=tok(D_MODEL),
        out_shape=jax.ShapeDtypeStruct((n_tok, D_MODEL), F32),
        compiler_params=pltpu.CompilerParams(dimension_semantics=("parallel",),
                                             vmem_limit_bytes=VMEM_LIMIT),
        name="mlp_ple",
    )(x2d, p2d, g_pre, w_up, w_down, g_post, w_ple, w_gate, g_ple)


def _layer(x, p, layer_idx, prm):
    (norm_mix_pre, w_in, w_fourier, w_attn, w_out, lq1, lk1, lq2, lk2, subln,
     norm_mix_post, norm_mlp_pre, w_up, w_down, norm_mlp_post, w_ple, w_gate, norm_ple_post) = prm
    b, seq, _ = x.shape
    n_tok = b * seq
    row = lambda g: g.reshape(1, -1).astype(F32)
    x2d = x.reshape(n_tok, D_MODEL)

    fa, fb, q1, q2, k, v, gf, ga = _in_proj(x2d, row(norm_mix_pre), *w_in, seq, tm=512)
    if seq <= DENSE_DFT_MAX_SEQ:
        y2 = _seq_dft_dense(fa, fb, b, seq)
    else:
        y2 = _seq_dft(fa, fb, b, seq, tcols=DFT_STAGE1_BLOCK_ELEMS * SEQ_DFT_N2 // seq, group=8)

    lam_init = 0.8 - 0.6 * math.exp(-0.3 * layer_idx)
    lam = _lambda(lam_init, lq1, lk1, lq2, lk2)
    att = _diff_attn(lam, q1, q2, k, v, row(subln),
                     1.0 - lam_init, tq=512, tk=1024, unroll=4, fin_group=4)

    x1 = _merge(x2d, y2, att.reshape(n_tok, ATTN_WIDTH), gf, ga, w_fourier, w_attn, w_out,
                row(norm_mix_post), tm=1024)
    out = _mlp_ple(x1, p.reshape(n_tok, PLE_DIM), row(norm_mlp_pre), w_up, w_down,
                   row(norm_mlp_post), w_ple, w_gate, row(norm_ple_post), tm=1024)
    return out.reshape(b, seq, D_MODEL)


def _prepare_weights(w_in, w_fourier, w_attn, w_out, w_up, w_down, w_ple, w_gate):
    w_in = w_in.astype(BF16)
    w_in = (_fold_chan_dft(w_in[:, :FOURIER_WIDTH]), w_in)
    return (w_in,) + tuple(w.astype(BF16) for w in (w_fourier, w_attn, w_out, w_up, w_down, w_ple, w_gate))


def kernel(x_prompt, x_sample, p_prompt, p_sample, norm_mix_pre, w_in, w_fourier, w_attn, w_out,
           lambda_q1, lambda_k1, lambda_q2, lambda_k2, subln, norm_mix_post, norm_mlp_pre,
           w_up, w_down, norm_mlp_post, w_ple, w_ple_gate, norm_ple_post):
    y_prompt, y_sample = x_prompt, x_sample
    for i in range(w_in.shape[0]):
        wi, wf, wa, wo, wu, wd, wp, wg = _prepare_weights(
            w_in[i], w_fourier[i], w_attn[i], w_out[i], w_up[i], w_down[i], w_ple[i], w_ple_gate[i])
        prm = (norm_mix_pre[i], wi, wf, wa, wo,
               lambda_q1[i], lambda_k1[i], lambda_q2[i], lambda_k2[i], subln[i],
               norm_mix_post[i], norm_mlp_pre[i], wu, wd, norm_mlp_post[i], wp, wg, norm_ple_post[i])
        y_prompt = _layer(y_prompt, p_prompt[i], i, prm)
        y_sample = _layer(y_sample, p_sample[i], i, prm)
    return (y_prompt, y_sample)
```
